```python
import math
import jax, jax.numpy as jnp
from jax import lax
import numpy as np

D_MODEL = 1024
BATCH = 16
SEQ = 4096
DEPTH = 1

CHUNK = 64
Q_BLOCK = 128
D_MIX = D_MODEL
W_ATTN = D_MIX // 2
W_RWKV = D_MIX - W_ATTN
ATTN_HEAD_DIM = 64
ATTN_HEADS = W_ATTN // (2 * ATTN_HEAD_DIM)
RWKV_HEAD_DIM = 64
RWKV_HEADS = W_RWKV // RWKV_HEAD_DIM
DECAY_LORA = max(32, int(round(1.8 * D_MODEL ** 0.5 / 32)) * 32)
AAA_LORA = max(32, int(round(1.8 * D_MODEL ** 0.5 / 32)) * 32)
RMS_EPS = 1e-6
SUBLN_EPS = 1e-5
GN_EPS = 1e-5 * RWKV_HEAD_DIM
SPLITS = (W_ATTN, W_ATTN, W_ATTN, W_ATTN,
          W_RWKV, W_RWKV, W_RWKV, DECAY_LORA, AAA_LORA, W_RWKV)
D_IN = sum(SPLITS)
SHIFT_WIDTH = 3 * W_RWKV + DECAY_LORA + AAA_LORA

kernel_name = "hymba_diffattn_rwkv7_chunk_causal"


def lambda_init(layer_idx):
    return 0.8 - 0.6 * math.exp(-0.3 * layer_idx)


def rms_norm(x, g, eps):
    x32 = x.astype(jnp.float32)
    y = x32 * lax.rsqrt(jnp.mean(x32 * x32, axis=-1, keepdims=True) + eps)
    return (y * g.astype(jnp.float32)).astype(x.dtype)


def token_shift(z, mu):
    prev = jnp.pad(z, ((0, 0), (1, 0), (0, 0)))[:, :-1]
    return z + (prev - z) * mu


def diff_attention(q1, q2, k1, k2, v, lam):
    seq = q1.shape[2]
    scale = ATTN_HEAD_DIM ** -0.5
    outs = []
    for blk in range(seq // Q_BLOCK):
        lo = blk * Q_BLOCK
        hi = lo + Q_BLOCK
        q_chunk = (lo + jnp.arange(Q_BLOCK)) // CHUNK
        k_chunk = jnp.arange(hi) // CHUNK
        mask = k_chunk[None, :] <= q_chunk[:, None]

        def probs(q, k):
            s = jnp.einsum('bhqd,bhkd->bhqk', q[:, :, lo:hi], k[:, :, :hi]).astype(jnp.float32) * scale
            return jax.nn.softmax(jnp.where(mask, s, -jnp.inf), axis=-1)

        p = probs(q1, k1) - lam * probs(q2, k2)
        outs.append(jnp.einsum('bhqk,bhkd->bhqd', p.astype(v.dtype), v[:, :, :hi]))
    return jnp.concatenate(outs, axis=2)


def rwkv7_recurrence(r, w, k, v, kk, a):
    bsz, _, heads, n = r.shape

    def step(state, inp):
        r_t, w_t, k_t, v_t, kk_t, a_t = inp
        sa = jnp.einsum('bhvk,bhk->bhv', state, -kk_t)
        state = (state * w_t[:, :, None, :]
                 + sa[..., None] * (kk_t * a_t)[:, :, None, :]
                 + v_t[..., None] * k_t[:, :, None, :])
        y_t = jnp.einsum('bhvk,bhk->bhv', state, r_t)
        return state, y_t

    xs = tuple(t.transpose(1, 0, 2, 3) for t in (r, w, k, v, kk, a))
    init = jnp.zeros((bsz, heads, n, n), jnp.float32)
    _, ys = lax.scan(step, init, xs)
    return ys.transpose(1, 0, 2, 3)


def setup_inputs(seed: int = 0) -> dict:
    key = jax.random.key(seed)
    ks = jax.random.split(key, 24)
    f32 = jnp.float32
    n = lambda k, shape: jax.random.normal(k, shape, f32)
    return {
        "x": n(ks[0], (BATCH, SEQ, D_MODEL)),
        "norm_g": 1.0 + 0.02 * n(ks[1], (DEPTH, D_MODEL)),
        "w_in": n(ks[2], (DEPTH, D_MODEL, D_IN)) * D_MODEL ** -0.5,
        "lambda_q1": 0.1 * n(ks[3], (DEPTH, ATTN_HEAD_DIM)),
        "lambda_k1": 0.1 * n(ks[4], (DEPTH, ATTN_HEAD_DIM)),
        "lambda_q2": 0.1 * n(ks[5], (DEPTH, ATTN_HEAD_DIM)),
        "lambda_k2": 0.1 * n(ks[6], (DEPTH, ATTN_HEAD_DIM)),
        "subln_g": 1.0 + 0.02 * n(ks[7], (DEPTH, 2 * ATTN_HEAD_DIM)),
        "shift_mu": jax.random.uniform(ks[8], (DEPTH, SHIFT_WIDTH), f32),
        "w0": jax.random.uniform(ks[9], (DEPTH, W_RWKV), f32, -3.0, 0.0),
        "w2": 0.5 * n(ks[10], (DEPTH, DECAY_LORA, W_RWKV)) * DECAY_LORA ** -0.5,
        "a0": 0.1 * n(ks[11], (DEPTH, W_RWKV)),
        "a2": n(ks[12], (DEPTH, AAA_LORA, W_RWKV)) * AAA_LORA ** -0.5,
        "k_k": 0.85 + 0.05 * n(ks[13], (DEPTH, W_RWKV)),
        "k_a": 1.0 + 0.05 * n(ks[14], (DEPTH, W_RWKV)),
        "r_k": 0.1 * n(ks[15], (DEPTH, RWKV_HEADS, RWKV_HEAD_DIM)),
        "ln_x_g": 1.0 + 0.02 * n(ks[16], (DEPTH, W_RWKV)),
        "ln_x_b": 0.02 * n(ks[17], (DEPTH, W_RWKV)),
        "w_out": n(ks[18], (DEPTH, D_MIX, D_MODEL)) * D_MIX ** -0.5,
        "final_g": 1.0 + 0.02 * n(ks[19], (D_MODEL,)),
    }


def reference(x, norm_g, w_in, lambda_q1, lambda_k1, lambda_q2, lambda_k2, subln_g,
              shift_mu, w0, w2, a0, a2, k_k, k_a, r_k, ln_x_g, ln_x_b, w_out, final_g):
    f32 = jnp.float32
    bsz, seq, _ = x.shape
    a_idx = [W_ATTN, 2 * W_ATTN, 3 * W_ATTN]
    b_idx = [W_RWKV, 2 * W_RWKV, 3 * W_RWKV, 3 * W_RWKV + DECAY_LORA]
    off_b = 4 * W_ATTN
    hd_b = lambda t: t.reshape(bsz, seq, RWKV_HEADS, RWKV_HEAD_DIM)
    h = x
    for l in range(DEPTH):
        xn = rms_norm(h, norm_g[l], RMS_EPS)
        proj = xn @ w_in[l]

        q, k_a_in, v_a, g_a = jnp.split(proj[..., :off_b], a_idx, axis=-1)
        q = q.reshape(bsz, seq, ATTN_HEADS, 2, ATTN_HEAD_DIM).transpose(3, 0, 2, 1, 4)
        k_att = k_a_in.reshape(bsz, seq, ATTN_HEADS, 2, ATTN_HEAD_DIM).transpose(3, 0, 2, 1, 4)
        v_a = v_a.reshape(bsz, seq, ATTN_HEADS, 2 * ATTN_HEAD_DIM).transpose(0, 2, 1, 3)
        lam_init = lambda_init(l)
        lam = (jnp.exp(jnp.sum(lambda_q1[l].astype(f32) * lambda_k1[l].astype(f32)))
               - jnp.exp(jnp.sum(lambda_q2[l].astype(f32) * lambda_k2[l].astype(f32)))
               + lam_init)
        o_a = diff_attention(q[0], q[1], k_att[0], k_att[1], v_a, lam)
        o_a = rms_norm(o_a, subln_g[l], SUBLN_EPS) * (1.0 - lam_init)
        o_a = o_a.transpose(0, 2, 1, 3).reshape(bsz, seq, W_ATTN) * jax.nn.silu(g_a)

        rwkv_in = token_shift(proj[..., off_b:off_b + SHIFT_WIDTH], shift_mu[l])
        r, k_b, v_b, wd, ad = jnp.split(rwkv_in, b_idx, axis=-1)
        g_b = proj[..., off_b + SHIFT_WIDTH:]
        w_log = -jax.nn.softplus(-(w0[l] + jnp.tanh(wd) @ w2[l]).astype(f32)) - 0.5
        decay = jnp.exp(-jnp.exp(w_log))
        a = jax.nn.sigmoid((a0[l] + ad @ a2[l]).astype(f32))
        kk = hd_b((k_b * k_k[l]).astype(f32))
        kk = kk / jnp.maximum(jnp.linalg.norm(kk, axis=-1, keepdims=True), 1e-12)
        k_mod = k_b.astype(f32) * (1.0 + (a - 1.0) * k_a[l].astype(f32))
        r_h, k_h, v_h = hd_b(r.astype(f32)), hd_b(k_mod), hd_b(v_b.astype(f32))
        y = rwkv7_recurrence(r_h, hd_b(decay), k_h, v_h, kk, hd_b(a))
        mu = jnp.mean(y, axis=-1, keepdims=True)
        var = jnp.mean(jnp.square(y - mu), axis=-1, keepdims=True)
        y = ((y - mu) * lax.rsqrt(var + GN_EPS)).reshape(bsz, seq, W_RWKV)
        y = y * ln_x_g[l].astype(f32) + ln_x_b[l].astype(f32)
        bonus = jnp.sum(r_h * k_h * r_k[l].astype(f32), axis=-1, keepdims=True) * v_h
        o_b = (y + bonus.reshape(bsz, seq, W_RWKV)).astype(x.dtype) * jax.nn.silu(g_b)

        mix = jnp.concatenate([o_a, o_b], axis=-1)
        h = h + mix @ w_out[l]
    return rms_norm(h, final_g, RMS_EPS)
```

```python
import functools
import math

import jax
import jax.numpy as jnp
from jax import lax
from jax.experimental import pallas as pl
from jax.experimental.pallas import tpu as pltpu

F32 = jnp.float32
BF16 = jnp.bfloat16

D_MODEL = 1024
CHUNK = 64
W_ATTN = D_MODEL // 2
W_RWKV = D_MODEL - W_ATTN
ATTN_HEAD_DIM = 64
ATTN_HEADS = W_ATTN // (2 * ATTN_HEAD_DIM)
RWKV_HEAD_DIM = 64
RWKV_HEADS = W_RWKV // RWKV_HEAD_DIM
LORA = 64
RMS_EPS = 1e-6
SUBLN_EPS = 1e-5
GN_EPS = 1e-5 * RWKV_HEAD_DIM
A_COLS = 4 * W_ATTN
SHIFT_WIDTH = 3 * W_RWKV + 2 * LORA
B_COLS = SHIFT_WIDTH + W_RWKV
D_IN = A_COLS + B_COLS

LANES = 128
PAIRS = W_RWKV // LANES
VMEM_LIMIT = 56 * 1024 * 1024


def _lambda_init(layer_idx):
    return 0.8 - 0.6 * math.exp(-0.3 * layer_idx)


def _dot(a, b):
    return jnp.dot(a, b, preferred_element_type=F32)


def _dot_nt(a, b):
    return lax.dot_general(a, b, (((1,), (1,)), ((), ())), preferred_element_type=F32)


def _inproj_kernel(x_ref, g_ref, wa_ref, wb_ref, oa_ref, ob_ref):
    x = x_ref[...]
    ms = jnp.mean(x * x, axis=-1, keepdims=True)
    xn = (x * lax.rsqrt(ms + RMS_EPS) * g_ref[...]).astype(BF16)
    oa_ref[...] = _dot(xn, wa_ref[...]).astype(oa_ref.dtype)
    ob_ref[...] = _dot(xn, wb_ref[...]).astype(ob_ref.dtype)


def _inproj(x2d, norm_g, w_a, w_b, tm):
    m = x2d.shape[0]
    return pl.pallas_call(
        _inproj_kernel,
        grid=(m // tm,),
        in_specs=[
            pl.BlockSpec((tm, D_MODEL), lambda i: (i, 0)),
            pl.BlockSpec((1, D_MODEL), lambda i: (0, 0)),
            pl.BlockSpec((D_MODEL, A_COLS), lambda i: (0, 0)),
            pl.BlockSpec((D_MODEL, B_COLS), lambda i: (0, 0)),
        ],
        out_specs=[
            pl.BlockSpec((tm, A_COLS), lambda i: (i, 0)),
            pl.BlockSpec((tm, B_COLS), lambda i: (i, 0)),
        ],
        out_shape=[
            jax.ShapeDtypeStruct((m, A_COLS), BF16),
            jax.ShapeDtypeStruct((m, B_COLS), BF16),
        ],
        compiler_params=pltpu.CompilerParams(
            dimension_semantics=("parallel",), vmem_limit_bytes=VMEM_LIMIT),
        name="inproj",
    )(x2d, norm_g, w_a, w_b)


def _attn_kernel(lam_ref, q_ref, k_ref, v_ref, g_ref, sg_ref, o_ref, *, tq, lam_init):
    i = pl.program_id(2)
    lane = lax.broadcasted_iota(jnp.int32, (1, LANES), 1)
    first = lane < ATTN_HEAD_DIM

    lp = lam_ref[...]
    l1 = jnp.sum(lp[0:1] * lp[1:2], axis=-1, keepdims=True)
    l2 = jnp.sum(lp[2:3] * lp[3:4], axis=-1, keepdims=True)
    lam = jnp.exp(l1) - jnp.exp(l2) + lam_init

    q = q_ref[0] * (ATTN_HEAD_DIM ** -0.5)
    zero = jnp.zeros_like(q)
    qs = jnp.concatenate([jnp.where(first, q, zero), jnp.where(first, zero, q)], axis=0)

    def block(j, carry, masked):
        m, l, acc = carry
        start = pl.multiple_of(j * tq, tq)
        kb = k_ref[0, pl.ds(start, tq), :]
        vb = v_ref[0, pl.ds(start, tq), :]
        s = _dot_nt(qs, kb)
        if masked:
            rq = lax.broadcasted_iota(jnp.int32, (2 * tq, tq), 0)
            rq = jnp.where(rq >= tq, rq - tq, rq) // CHUNK
            ck = lax.broadcasted_iota(jnp.int32, (2 * tq, tq), 1) // CHUNK
            s = jnp.where(ck <= rq, s, -jnp.inf)
        m_new = jnp.maximum(m, jnp.max(s, axis=-1, keepdims=True))
        p = jnp.exp(s - m_new)
        alpha = jnp.exp(m - m_new)
        l = alpha * l + jnp.sum(p, axis=-1, keepdims=True)
        acc = alpha * acc + _dot(p.astype(BF16), vb)
        return m_new, l, acc

    init = (jnp.full((2 * tq, 1), -jnp.inf, F32), jnp.zeros((2 * tq, 1), F32),
            jnp.zeros((2 * tq, LANES), F32))
    carry = lax.fori_loop(0, i, lambda j, c: block(j, c, False), init)
    m, l, acc = block(i, carry, True)

    o = acc / l
    o = o[:tq] - lam * o[tq:]
    ms = jnp.mean(o * o, axis=-1, keepdims=True)
    o = o * lax.rsqrt(ms + SUBLN_EPS) * sg_ref[...] * (1.0 - lam_init)
    g = g_ref[0].astype(F32)
    o_ref[0] = (o * (g / (1.0 + jnp.exp(-g)))).astype(o_ref.dtype)


def _attention(proj_a, lam_params, subln_g, bsz, seq, tq, lam_init):
    nh = ATTN_HEADS
    kernel = functools.partial(_attn_kernel, tq=tq, lam_init=lam_init)
    return pl.pallas_call(
        kernel,
        grid=(bsz, nh, seq // tq),
        in_specs=[
            pl.BlockSpec((4, ATTN_HEAD_DIM), lambda b, h, i: (0, 0)),
            pl.BlockSpec((1, tq, LANES), lambda b, h, i: (b, i, h)),
            pl.BlockSpec((1, seq, LANES), lambda b, h, i: (b, 0, nh + h)),
            pl.BlockSpec((1, seq, LANES), lambda b, h, i: (b, 0, 2 * nh + h)),
            pl.BlockSpec((1, tq, LANES), lambda b, h, i: (b, i, 3 * nh + h)),
            pl.BlockSpec((1, LANES), lambda b, h, i: (0, 0)),
        ],
        out_specs=pl.BlockSpec((1, tq, LANES), lambda b, h, i: (b, i, h)),
        out_shape=jax.ShapeDtypeStruct((bsz, seq, W_ATTN), BF16),
        compiler_params=pltpu.CompilerParams(
            dimension_semantics=("parallel", "parallel", "arbitrary"),
            vmem_limit_bytes=VMEM_LIMIT),
        name="diff_attn",
    )(lam_params, proj_a, proj_a, proj_a, proj_a, subln_g)


def _split3(x):
    hi = x.astype(BF16)
    r1 = x - hi.astype(F32)
    mid = r1.astype(BF16)
    lo = (r1 - mid.astype(F32)).astype(BF16)
    return hi, mid, lo


def _rwkv_kernel(pb_ref, mu_ref, w0_ref, w2_ref, a0_ref, a2_ref, kk_ref, ka_ref, rk_ref,
                 lng_ref, lnb_ref, o_ref, prev_ref, state_ref):
    c = CHUNK
    hd = RWKV_HEAD_DIM

    @pl.when(pl.program_id(1) == 0)
    def _():
        prev_ref[...] = jnp.zeros_like(prev_ref)
        state_ref[...] = jnp.zeros_like(state_ref)

    z = pb_ref[0].astype(F32)
    zs = z[:, :SHIFT_WIDTH]
    row = lax.broadcasted_iota(jnp.int32, (c, 1), 0)
    prev = jnp.where(row == 0, prev_ref[...], pltpu.roll(zs, 1, axis=0))
    prev_ref[...] = zs[c - 1:c, :]
    xs = zs + (prev - zs) * mu_ref[...]

    r = xs[:, 0:W_RWKV]
    kb = xs[:, W_RWKV:2 * W_RWKV]
    vb = xs[:, 2 * W_RWKV:3 * W_RWKV]
    lora_in = xs[:, 3 * W_RWKV:SHIFT_WIDTH]
    g = z[:, SHIFT_WIDTH:]

    u = w0_ref[...] + _dot(jnp.tanh(lora_in).astype(BF16), w2_ref[...])
    w_log = -(jnp.maximum(-u, 0.0) + jnp.log(1.0 + jnp.exp(-jnp.abs(u)))) - 0.5
    logw = -jnp.exp(w_log)
    a = 1.0 / (1.0 + jnp.exp(-(a0_ref[...] + _dot(lora_in.astype(BF16), a2_ref[...]))))

    lane4 = lax.broadcasted_iota(jnp.int32, (1, W_RWKV), 1)
    first4 = (lane4 % LANES) < hd

    def head_sum(x):
        outs = []
        for p in range(PAIRS):
            xp = x[:, p * LANES:(p + 1) * LANES]
            fp = first4[:, p * LANES:(p + 1) * LANES]
            s0 = jnp.sum(jnp.where(fp, xp, 0.0), axis=-1, keepdims=True)
            s1 = jnp.sum(jnp.where(fp, 0.0, xp), axis=-1, keepdims=True)
            outs.append(jnp.where(fp, s0, s1))
        return jnp.concatenate(outs, axis=1)

    kk = kb * kk_ref[...]
    kk = kk / jnp.maximum(jnp.sqrt(head_sum(kk * kk)), 1e-12)
    k_mod = kb * (1.0 + (a - 1.0) * ka_ref[...])
    bonus = head_sum(r * k_mod * rk_ref[...]) * vb

    ri = lax.broadcasted_iota(jnp.int32, (c, c), 0)
    ci = lax.broadcasted_iota(jnp.int32, (c, c), 1)
    tri = (ci <= ri).astype(BF16)
    hi, mid, lo = _split3(logw)
    cs = _dot(tri, hi) + _dot(tri, mid) + _dot(tri, lo)
    c_end = cs[c - 1:c, :]

    e_neg = jnp.exp(-cs)
    e_end = jnp.exp(c_end - cs)
    kka = kk * a
    at_all = -kk * jnp.exp(cs - logw)
    bt_all = kka * e_neg
    kt_all = k_mod * e_neg
    rt_all = r * jnp.exp(cs)
    bp_all = kka * e_end
    kp_all = k_mod * e_end
    pc_all = jnp.exp(c_end)

    lane = lax.broadcasted_iota(jnp.int32, (1, LANES), 1)
    first = lane < hd
    rowc = lax.broadcasted_iota(jnp.int32, (c, LANES), 0)
    colc = lax.broadcasted_iota(jnp.int32, (c, LANES), 1) % hd
    strict = colc < rowc
    incl = colc <= rowc
    r2 = lax.broadcasted_iota(jnp.int32, (LANES, LANES), 0)
    c2 = lax.broadcasted_iota(jnp.int32, (LANES, LANES), 1)
    same_head = (r2 // hd) == (c2 // hd)
    same_blk = (r2 // 16) == (c2 // 16)
    eye = r2 == c2

    def h0(x):
        return jnp.where(first, x, 0.0)

    def h1(x):
        return jnp.where(first, 0.0, x)

    zeros_cl = jnp.zeros((c, LANES), F32)

    for p in range(PAIRS):
        sl = slice(p * LANES, (p + 1) * LANES)
        at, bt, kt, rt = at_all[:, sl], bt_all[:, sl], kt_all[:, sl], rt_all[:, sl]
        bp, kp, vm, pc = bp_all[:, sl], kp_all[:, sl], vb[:, sl], pc_all[:, sl]

        ar = jnp.concatenate([at, rt], axis=0).astype(BF16)
        rh = jnp.concatenate([h0(bt), h0(kt), h1(kt), h1(bt)], axis=0).astype(BF16)
        mm = _dot_nt(ar, rh)
        m0, m1 = mm[:, :LANES], mm[:, LANES:]
        la0 = jnp.where(strict, m0[:c], 0.0)
        la1 = jnp.where(strict, m1[:c], 0.0)
        mr0 = jnp.where(incl, m0[c:], 0.0)
        mr1 = jnp.where(incl, m1[c:], 0.0)

        vm_sw = jnp.concatenate([h1(vm), h0(vm)], axis=0).astype(BF16)
        lak = jnp.where(first, la1, la0).astype(BF16)
        lakv = _dot(lak, vm_sw)

        lbd = jnp.concatenate([h0(la0), h1(la1)], axis=0)
        zc = jnp.concatenate([at, lakv], axis=1)
        first2 = jnp.concatenate([first, first], axis=1)
        x = jnp.concatenate([jnp.where(first2, zc, 0.0), jnp.where(first2, 0.0, zc)], axis=0)
        dblk = jnp.where(same_blk, lbd, 0.0)
        xx = jnp.concatenate([lbd - dblk, x], axis=1)
        dpow = dblk.astype(BF16)
        for lvl in range(4):
            xx = xx + _dot(dpow, xx.astype(BF16))
            if lvl < 3:
                dpow = _dot(dpow, dpow).astype(BF16)
        f = xx[:, :LANES].astype(BF16)
        x = xx[:, LANES:]
        x = x + _dot(f, x.astype(BF16))
        f2 = _dot(f, f).astype(BF16)
        x = x + _dot(f2, x.astype(BF16))
        tz = x[:c] + x[c:]
        w, u0 = tz[:, :LANES], tz[:, LANES:]

        lt = jnp.concatenate([bp, kp], axis=0).T.astype(BF16)
        rt2 = jnp.concatenate([jnp.concatenate([w, u0], axis=1),
                               jnp.concatenate([zeros_cl, vm], axis=1)], axis=0).astype(BF16)
        gh = _dot(lt, rt2)
        gmat = jnp.where(same_head, gh[:, :LANES], 0.0) + jnp.where(eye, pc, 0.0)
        hmat = jnp.where(same_head, gh[:, LANES:], 0.0)

        lq = jnp.concatenate([jnp.where(first, mr0, mr1), jnp.where(first, mr1, mr0)],
                             axis=1).astype(BF16)
        rq = jnp.concatenate([
            jnp.concatenate([h0(w), h0(u0)], axis=1),
            jnp.concatenate([h1(w), h1(u0)], axis=1),
            jnp.concatenate([zeros_cl, h1(vm)], axis=1),
            jnp.concatenate([zeros_cl, h0(vm)], axis=1)], axis=0).astype(BF16)
        qy = _dot(lq, rq)
        qh = rt + qy[:, :LANES]
        y0 = qy[:, LANES:]

        s_old = state_ref[p]
        gs = _dot(jnp.concatenate([gmat, qh], axis=0).astype(BF16), s_old.astype(BF16))
        state_ref[p] = gs[:LANES] + hmat
        y = gs[LANES:] + y0

        def pair_mean(t):
            s0 = jnp.sum(h0(t), axis=-1, keepdims=True)
            s1 = jnp.sum(h1(t), axis=-1, keepdims=True)
            return jnp.where(first, s0, s1) * (1.0 / hd)

        mu = pair_mean(y)
        d = y - mu
        var = pair_mean(d * d)
        yn = d * lax.rsqrt(var + GN_EPS) * lng_ref[:, sl] + lnb_ref[:, sl]
        gp = g[:, sl]
        o_ref[0, :, sl] = ((yn + bonus[:, sl]) * (gp / (1.0 + jnp.exp(-gp)))).astype(o_ref.dtype)


def _rwkv(proj_b, mu, w0, w2p, a0, a2p, k_k, k_a, r_k, ln_g, ln_b, bsz, seq):
    c = CHUNK
    vec = lambda n: pl.BlockSpec((1, n), lambda b, t: (0, 0))
    mat = pl.BlockSpec((LANES, W_RWKV), lambda b, t: (0, 0))
    return pl.pallas_call(
        _rwkv_kernel,
        grid=(bsz, seq // c),
        in_specs=[
            pl.BlockSpec((1, c, B_COLS), lambda b, t: (b, t, 0)),
            vec(SHIFT_WIDTH), vec(W_RWKV), mat, vec(W_RWKV), mat,
            vec(W_RWKV), vec(W_RWKV), vec(W_RWKV), vec(W_RWKV), vec(W_RWKV),
        ],
        out_specs=pl.BlockSpec((1, c, W_RWKV), lambda b, t: (b, t, 0)),
        out_shape=jax.ShapeDtypeStruct((bsz, seq, W_RWKV), BF16),
        scratch_shapes=[
            pltpu.VMEM((1, SHIFT_WIDTH), F32),
            pltpu.VMEM((PAIRS, LANES, LANES), F32),
        ],
        compiler_params=pltpu.CompilerParams(
            dimension_semantics=("parallel", "arbitrary"), vmem_limit_bytes=VMEM_LIMIT),
        name="rwkv7",
    )(proj_b, mu, w0, w2p, a0, a2p, k_k, k_a, r_k, ln_g, ln_b)


def _outproj_kernel(x_ref, oa_ref, ob_ref, wa_ref, wb_ref, fg_ref, o_ref, *, final):
    h = x_ref[...] + _dot(oa_ref[...], wa_ref[...]) + _dot(ob_ref[...], wb_ref[...])
    if final:
        ms = jnp.mean(h * h, axis=-1, keepdims=True)
        h = h * lax.rsqrt(ms + RMS_EPS) * fg_ref[...]
    o_ref[...] = h


def _outproj(x2d, oa, ob, w_oa, w_ob, final_g, tm, final):
    m = x2d.shape[0]
    return pl.pallas_call(
        functools.partial(_outproj_kernel, final=final),
        grid=(m // tm,),
        in_specs=[
            pl.BlockSpec((tm, D_MODEL), lambda i: (i, 0)),
            pl.BlockSpec((tm, W_ATTN), lambda i: (i, 0)),
            pl.BlockSpec((tm, W_RWKV), lambda i: (i, 0)),
            pl.BlockSpec((W_ATTN, D_MODEL), lambda i: (0, 0)),
            pl.BlockSpec((W_RWKV, D_MODEL), lambda i: (0, 0)),
            pl.BlockSpec((1, D_MODEL), lambda i: (0, 0)),
        ],
        out_specs=pl.BlockSpec((tm, D_MODEL), lambda i: (i, 0)),
        out_shape=jax.ShapeDtypeStruct((m, D_MODEL), F32),
        compiler_params=pltpu.CompilerParams(
            dimension_semantics=("parallel",), vmem_limit_bytes=VMEM_LIMIT),
        name="outproj",
    )(x2d, oa, ob, w_oa, w_ob, final_g)


def _pad_rows(w, top):
    zeros = jnp.zeros_like(w)
    return jnp.concatenate([w, zeros] if top else [zeros, w], axis=0)


def kernel(x, norm_g, w_in, lambda_q1, lambda_k1, lambda_q2, lambda_k2, subln_g,
           shift_mu, w0, w2, a0, a2, k_k, k_a, r_k, ln_x_g, ln_x_b, w_out, final_g):
    bsz, seq, d = x.shape
    depth = w_in.shape[0]
    assert d == D_MODEL and w_in.shape[2] == D_IN
    m = bsz * seq
    tm = min(512, m)
    tq = min(256, seq)
    assert m % tm == 0 and seq % tq == 0 and seq % CHUNK == 0

    row = lambda v: v.reshape(1, -1).astype(F32)
    h = x.reshape(m, d)
    for l in range(depth):
        proj_a, proj_b = _inproj(h, row(norm_g[l]), w_in[l, :, :A_COLS].astype(BF16),
                                 w_in[l, :, A_COLS:].astype(BF16), tm)
        lam_params = jnp.stack([lambda_q1[l], lambda_k1[l], lambda_q2[l], lambda_k2[l]]).astype(F32)
        o_a = _attention(proj_a.reshape(bsz, seq, A_COLS), lam_params, row(subln_g[l]),
                         bsz, seq, tq, _lambda_init(l))
        o_b = _rwkv(proj_b.reshape(bsz, seq, B_COLS), row(shift_mu[l]), row(w0[l]),
                    _pad_rows(w2[l], True).astype(BF16), row(a0[l]),
                    _pad_rows(a2[l], False).astype(BF16), row(k_k[l]), row(k_a[l]),
                    row(r_k[l]), row(ln_x_g[l]), row(ln_x_b[l]), bsz, seq)
        h = _outproj(h, o_a.reshape(m, W_ATTN), o_b.reshape(m, W_RWKV),
                     w_out[l, :W_ATTN].astype(BF16), w_out[l, W_ATTN:].astype(BF16),
                     row(final_g), tm, final=(l == depth - 1))
    return h.reshape(bsz, seq, d)
```

```python
import functools
import math

import jax
import jax.numpy as jnp
from jax import lax
from jax.experimental import pallas as pl
from jax.experimental.pallas import tpu as pltpu

F32 = jnp.float32
BF16 = jnp.bfloat16

D_MODEL = 1024
CHUNK = 64
W_ATTN = D_MODEL // 2
W_RWKV = D_MODEL - W_ATTN
ATTN_HEAD_DIM = 64
ATTN_HEADS = W_ATTN // (2 * ATTN_HEAD_DIM)
RWKV_HEAD_DIM = 64
RWKV_HEADS = W_RWKV // RWKV_HEAD_DIM
LORA = 64
RMS_EPS = 1e-6
SUBLN_EPS = 1e-5
GN_EPS = 1e-5 * RWKV_HEAD_DIM
A_COLS = 4 * W_ATTN
SHIFT_WIDTH = 3 * W_RWKV + 2 * LORA
B_COLS = SHIFT_WIDTH + W_RWKV
D_IN = A_COLS + B_COLS

LANES = 128
PAIRS = W_RWKV // LANES
VMEM_LIMIT = 56 * 1024 * 1024


def _lambda_init(layer_idx):
    return 0.8 - 0.6 * math.exp(-0.3 * layer_idx)


def _dot(a, b):
    return jnp.dot(a, b, preferred_element_type=F32)


def _dot_nt(a, b):
    return lax.dot_general(a, b, (((1,), (1,)), ((), ())), preferred_element_type=F32)


def _inproj_kernel(x_ref, g_ref, wa_ref, wb_ref, oa_ref, ob_ref):
    x = x_ref[...]
    ms = jnp.mean(x * x, axis=-1, keepdims=True)
    xn = (x * lax.rsqrt(ms + RMS_EPS) * g_ref[...]).astype(BF16)
    oa_ref[...] = _dot(xn, wa_ref[...]).astype(oa_ref.dtype)
    ob_ref[...] = _dot(xn, wb_ref[...]).astype(ob_ref.dtype)


def _inproj(x2d, norm_g, w_a, w_b, tm):
    m = x2d.shape[0]
    return pl.pallas_call(
        _inproj_kernel,
        grid=(m // tm,),
        in_specs=[
            pl.BlockSpec((tm, D_MODEL), lambda i: (i, 0)),
            pl.BlockSpec((1, D_MODEL), lambda i: (0, 0)),
            pl.BlockSpec((D_MODEL, A_COLS), lambda i: (0, 0)),
            pl.BlockSpec((D_MODEL, B_COLS), lambda i: (0, 0)),
        ],
        out_specs=[
            pl.BlockSpec((tm, A_COLS), lambda i: (i, 0)),
            pl.BlockSpec((tm, B_COLS), lambda i: (i, 0)),
        ],
        out_shape=[
            jax.ShapeDtypeStruct((m, A_COLS), BF16),
            jax.ShapeDtypeStruct((m, B_COLS), BF16),
        ],
        compiler_params=pltpu.CompilerParams(
            dimension_semantics=("parallel",), vmem_limit_bytes=VMEM_LIMIT),
        name="inproj",
    )(x2d, norm_g, w_a, w_b)


def _attn_kernel(lam_ref, q_ref, k_ref, v_ref, g_ref, sg_ref, o_ref, m_scr, l_scr, acc_scr,
                 *, tq, lam_init):
    i = pl.program_id(2)
    lane = lax.broadcasted_iota(jnp.int32, (1, LANES), 1)
    first = lane < ATTN_HEAD_DIM

    lp = lam_ref[...]
    l1 = jnp.sum(lp[0:1] * lp[1:2], axis=-1, keepdims=True)
    l2 = jnp.sum(lp[2:3] * lp[3:4], axis=-1, keepdims=True)
    lam = jnp.exp(l1) - jnp.exp(l2) + lam_init

    q = q_ref[0] * (ATTN_HEAD_DIM ** -0.5)
    zero = jnp.zeros_like(q)
    qs = jnp.concatenate([jnp.where(first, q, zero), jnp.where(first, zero, q)], axis=0)

    nt = tq // LANES

    def scores(j, masked):
        start = pl.multiple_of(j * tq, tq)
        s = _dot_nt(qs, k_ref[0, pl.ds(start, tq), :])
        if masked:
            rq = lax.broadcasted_iota(jnp.int32, (2 * tq, tq), 0)
            rq = jnp.where(rq >= tq, rq - tq, rq) // CHUNK
            ck = lax.broadcasted_iota(jnp.int32, (2 * tq, tq), 1) // CHUNK
            s = jnp.where(ck <= rq, s, -jnp.inf)
        return [s[:, t * LANES:(t + 1) * LANES] for t in range(nt)]

    def tile_max(tiles):
        return functools.reduce(jnp.maximum, tiles)

    m_scr[...] = tile_max(scores(i, True))

    def pass1(j, carry):
        m_scr[...] = jnp.maximum(m_scr[...], tile_max(scores(j, False)))
        return carry

    lax.fori_loop(0, i, pass1, 0)
    mb = jnp.broadcast_to(jnp.max(m_scr[...], axis=-1, keepdims=True), (2 * tq, LANES))

    def probs_v(j, masked):
        start = pl.multiple_of(j * tq, tq)
        ps = [jnp.exp(t - mb) for t in scores(j, masked)]
        pv = _dot(jnp.concatenate([p.astype(BF16) for p in ps], axis=1),
                  v_ref[0, pl.ds(start, tq), :])
        return functools.reduce(jnp.add, ps), pv

    l_scr[...], acc_scr[...] = probs_v(i, True)

    def pass2(j, carry):
        lsum, pv = probs_v(j, False)
        l_scr[...] += lsum
        acc_scr[...] += pv
        return carry

    lax.fori_loop(0, i, pass2, 0)
    l = jnp.sum(l_scr[...], axis=-1, keepdims=True)

    o = acc_scr[...] / l
    o = o[:tq] - lam * o[tq:]
    ms = jnp.mean(o * o, axis=-1, keepdims=True)
    o = o * lax.rsqrt(ms + SUBLN_EPS) * sg_ref[...] * (1.0 - lam_init)
    g = g_ref[0].astype(F32)
    o_ref[0] = (o * (g / (1.0 + jnp.exp(-g)))).astype(o_ref.dtype)


def _attention(proj_a, lam_params, subln_g, bsz, seq, tq, lam_init):
    nh = ATTN_HEADS
    kernel = functools.partial(_attn_kernel, tq=tq, lam_init=lam_init)
    return pl.pallas_call(
        kernel,
        grid=(bsz, nh, seq // tq),
        in_specs=[
            pl.BlockSpec((4, ATTN_HEAD_DIM), lambda b, h, i: (0, 0)),
            pl.BlockSpec((1, tq, LANES), lambda b, h, i: (b, i, h)),
            pl.BlockSpec((1, seq, LANES), lambda b, h, i: (b, 0, nh + h)),
            pl.BlockSpec((1, seq, LANES), lambda b, h, i: (b, 0, 2 * nh + h)),
            pl.BlockSpec((1, tq, LANES), lambda b, h, i: (b, i, 3 * nh + h)),
            pl.BlockSpec((1, LANES), lambda b, h, i: (0, 0)),
        ],
        out_specs=pl.BlockSpec((1, tq, LANES), lambda b, h, i: (b, i, h)),
        out_shape=jax.ShapeDtypeStruct((bsz, seq, W_ATTN), BF16),
        scratch_shapes=[pltpu.VMEM((2 * tq, LANES), F32)] * 3,
        compiler_params=pltpu.CompilerParams(
            dimension_semantics=("parallel", "parallel", "arbitrary"),
            vmem_limit_bytes=VMEM_LIMIT),
        name="diff_attn",
    )(lam_params, proj_a, proj_a, proj_a, proj_a, subln_g)


def _split3(x):
    hi = x.astype(BF16)
    r1 = x - hi.astype(F32)
    mid = r1.astype(BF16)
    lo = (r1 - mid.astype(F32)).astype(BF16)
    return hi, mid, lo


def _rwkv_kernel(pb_ref, mu_ref, w0_ref, w2_ref, a0_ref, a2_ref, kk_ref, ka_ref, rk_ref,
                 lng_ref, lnb_ref, o_ref, prev_ref, state_ref):
    c = CHUNK
    hd = RWKV_HEAD_DIM

    @pl.when(pl.program_id(1) == 0)
    def _():
        prev_ref[...] = jnp.zeros_like(prev_ref)
        state_ref[...] = jnp.zeros_like(state_ref)

    z = pb_ref[0].astype(F32)
    zs = z[:, :SHIFT_WIDTH]
    row = lax.broadcasted_iota(jnp.int32, (c, 1), 0)
    prev = jnp.where(row == 0, prev_ref[...], pltpu.roll(zs, 1, axis=0))
    prev_ref[...] = zs[c - 1:c, :]
    xs = zs + (prev - zs) * mu_ref[...]

    r = xs[:, 0:W_RWKV]
    kb = xs[:, W_RWKV:2 * W_RWKV]
    vb = xs[:, 2 * W_RWKV:3 * W_RWKV]
    lora_in = xs[:, 3 * W_RWKV:SHIFT_WIDTH]
    g = z[:, SHIFT_WIDTH:]

    u = w0_ref[...] + _dot(jnp.tanh(lora_in).astype(BF16), w2_ref[...])
    w_log = -(jnp.maximum(-u, 0.0) + jnp.log(1.0 + jnp.exp(-jnp.abs(u)))) - 0.5
    logw = -jnp.exp(w_log)
    a = 1.0 / (1.0 + jnp.exp(-(a0_ref[...] + _dot(lora_in.astype(BF16), a2_ref[...]))))

    lane4 = lax.broadcasted_iota(jnp.int32, (1, W_RWKV), 1)
    first4 = (lane4 % LANES) < hd

    def head_sum(x):
        outs = []
        for p in range(PAIRS):
            xp = x[:, p * LANES:(p + 1) * LANES]
            fp = first4[:, p * LANES:(p + 1) * LANES]
            s0 = jnp.sum(jnp.where(fp, xp, 0.0), axis=-1, keepdims=True)
            s1 = jnp.sum(jnp.where(fp, 0.0, xp), axis=-1, keepdims=True)
            outs.append(jnp.where(fp, s0, s1))
        return jnp.concatenate(outs, axis=1)

    kk = kb * kk_ref[...]
    kk = kk / jnp.maximum(jnp.sqrt(head_sum(kk * kk)), 1e-12)
    k_mod = kb * (1.0 + (a - 1.0) * ka_ref[...])
    bonus = head_sum(r * k_mod * rk_ref[...]) * vb

    ri = lax.broadcasted_iota(jnp.int32, (c, c), 0)
    ci = lax.broadcasted_iota(jnp.int32, (c, c), 1)
    tri = (ci <= ri).astype(BF16)
    hi, mid, lo = _split3(logw)
    cs = _dot(tri, hi) + _dot(tri, mid) + _dot(tri, lo)
    c_end = cs[c - 1:c, :]

    e_neg = jnp.exp(-cs)
    e_end = jnp.exp(c_end - cs)
    kka = kk * a
    at_all = -kk * jnp.exp(cs - logw)
    bt_all = kka * e_neg
    kt_all = k_mod * e_neg
    rt_all = r * jnp.exp(cs)
    bp_all = kka * e_end
    kp_all = k_mod * e_end
    pc_all = jnp.exp(c_end)

    lane = lax.broadcasted_iota(jnp.int32, (1, LANES), 1)
    first = lane < hd
    rowc = lax.broadcasted_iota(jnp.int32, (c, LANES), 0)
    colc = lax.broadcasted_iota(jnp.int32, (c, LANES), 1) % hd
    strict = colc < rowc
    incl = colc <= rowc
    r2 = lax.broadcasted_iota(jnp.int32, (LANES, LANES), 0)
    c2 = lax.broadcasted_iota(jnp.int32, (LANES, LANES), 1)
    same_head = (r2 // hd) == (c2 // hd)
    same_blk = (r2 // 16) == (c2 // 16)
    eye = r2 == c2

    def h0(x):
        return jnp.where(first, x, 0.0)

    def h1(x):
        return jnp.where(first, 0.0, x)

    zeros_cl = jnp.zeros((c, LANES), F32)

    for p in range(PAIRS):
        sl = slice(p * LANES, (p + 1) * LANES)
        at, bt, kt, rt = at_all[:, sl], bt_all[:, sl], kt_all[:, sl], rt_all[:, sl]
        bp, kp, vm, pc = bp_all[:, sl], kp_all[:, sl], vb[:, sl], pc_all[:, sl]

        ar = jnp.concatenate([at, rt], axis=0).astype(BF16)
        rh = jnp.concatenate([h0(bt), h0(kt), h1(kt), h1(bt)], axis=0).astype(BF16)
        mm = _dot_nt(ar, rh)
        m0, m1 = mm[:, :LANES], mm[:, LANES:]
        la0 = jnp.where(strict, m0[:c], 0.0)
        la1 = jnp.where(strict, m1[:c], 0.0)
        mr0 = jnp.where(incl, m0[c:], 0.0)
        mr1 = jnp.where(incl, m1[c:], 0.0)

        vm_sw = jnp.concatenate([h1(vm), h0(vm)], axis=0).astype(BF16)
        lak = jnp.where(first, la1, la0).astype(BF16)
        lakv = _dot(lak, vm_sw)

        lbd = jnp.concatenate([h0(la0), h1(la1)], axis=0)
        zc = jnp.concatenate([at, lakv], axis=1)
        first2 = jnp.concatenate([first, first], axis=1)
        x = jnp.concatenate([jnp.where(first2, zc, 0.0), jnp.where(first2, 0.0, zc)], axis=0)
        dblk = jnp.where(same_blk, lbd, 0.0)
        xx = jnp.concatenate([lbd - dblk, x], axis=1)
        dpow = dblk.astype(BF16)
        for lvl in range(4):
            xx = xx + _dot(dpow, xx.astype(BF16))
            if lvl < 3:
                dpow = _dot(dpow, dpow).astype(BF16)
        f = xx[:, :LANES].astype(BF16)
        x = xx[:, LANES:]
        x = x + _dot(f, x.astype(BF16))
        f2 = _dot(f, f).astype(BF16)
        x = x + _dot(f2, x.astype(BF16))
        tz = x[:c] + x[c:]
        w, u0 = tz[:, :LANES], tz[:, LANES:]

        lt = jnp.concatenate([bp, kp], axis=0).T.astype(BF16)
        rt2 = jnp.concatenate([jnp.concatenate([w, u0], axis=1),
                               jnp.concatenate([zeros_cl, vm], axis=1)], axis=0).astype(BF16)
        gh = _dot(lt, rt2)
        gmat = jnp.where(same_head, gh[:, :LANES], 0.0) + jnp.where(eye, pc, 0.0)
        hmat = jnp.where(same_head, gh[:, LANES:], 0.0)

        lq = jnp.concatenate([jnp.where(first, mr0, mr1), jnp.where(first, mr1, mr0)],
                             axis=1).astype(BF16)
        rq = jnp.concatenate([
            jnp.concatenate([h0(w), h0(u0)], axis=1),
            jnp.concatenate([h1(w), h1(u0)], axis=1),
            jnp.concatenate([zeros_cl, h1(vm)], axis=1),
            jnp.concatenate([zeros_cl, h0(vm)], axis=1)], axis=0).astype(BF16)
        qy = _dot(lq, rq)
        qh = rt + qy[:, :LANES]
        y0 = qy[:, LANES:]

        s_old = state_ref[p]
        gs = _dot(jnp.concatenate([gmat, qh], axis=0).astype(BF16), s_old.astype(BF16))
        state_ref[p] = gs[:LANES] + hmat
        y = gs[LANES:] + y0

        def pair_mean(t):
            s0 = jnp.sum(h0(t), axis=-1, keepdims=True)
            s1 = jnp.sum(h1(t), axis=-1, keepdims=True)
            return jnp.where(first, s0, s1) * (1.0 / hd)

        mu = pair_mean(y)
        d = y - mu
        var = pair_mean(d * d)
        yn = d * lax.rsqrt(var + GN_EPS) * lng_ref[:, sl] + lnb_ref[:, sl]
        gp = g[:, sl]
        o_ref[0, :, sl] = ((yn + bonus[:, sl]) * (gp / (1.0 + jnp.exp(-gp)))).astype(o_ref.dtype)


def _rwkv(proj_b, mu, w0, w2p, a0, a2p, k_k, k_a, r_k, ln_g, ln_b, bsz, seq):
    c = CHUNK
    vec = lambda n: pl.BlockSpec((1, n), lambda b, t: (0, 0))
    mat = pl.BlockSpec((LANES, W_RWKV), lambda b, t: (0, 0))
    return pl.pallas_call(
        _rwkv_kernel,
        grid=(bsz, seq // c),
        in_specs=[
            pl.BlockSpec((1, c, B_COLS), lambda b, t: (b, t, 0)),
            vec(SHIFT_WIDTH), vec(W_RWKV), mat, vec(W_RWKV), mat,
            vec(W_RWKV), vec(W_RWKV), vec(W_RWKV), vec(W_RWKV), vec(W_RWKV),
        ],
        out_specs=pl.BlockSpec((1, c, W_RWKV), lambda b, t: (b, t, 0)),
        out_shape=jax.ShapeDtypeStruct((bsz, seq, W_RWKV), BF16),
        scratch_shapes=[
            pltpu.VMEM((1, SHIFT_WIDTH), F32),
            pltpu.VMEM((PAIRS, LANES, LANES), F32),
        ],
        compiler_params=pltpu.CompilerParams(
            dimension_semantics=("parallel", "arbitrary"), vmem_limit_bytes=VMEM_LIMIT),
        name="rwkv7",
    )(proj_b, mu, w0, w2p, a0, a2p, k_k, k_a, r_k, ln_g, ln_b)


def _outproj_kernel(x_ref, oa_ref, ob_ref, wa_ref, wb_ref, fg_ref, o_ref, *, final):
    h = x_ref[...] + _dot(oa_ref[...], wa_ref[...]) + _dot(ob_ref[...], wb_ref[...])
    if final:
        ms = jnp.mean(h * h, axis=-1, keepdims=True)
        h = h * lax.rsqrt(ms + RMS_EPS) * fg_ref[...]
    o_ref[...] = h


def _outproj(x2d, oa, ob, w_oa, w_ob, final_g, tm, final):
    m = x2d.shape[0]
    return pl.pallas_call(
        functools.partial(_outproj_kernel, final=final),
        grid=(m // tm,),
        in_specs=[
            pl.BlockSpec((tm, D_MODEL), lambda i: (i, 0)),
            pl.BlockSpec((tm, W_ATTN), lambda i: (i, 0)),
            pl.BlockSpec((tm, W_RWKV), lambda i: (i, 0)),
            pl.BlockSpec((W_ATTN, D_MODEL), lambda i: (0, 0)),
            pl.BlockSpec((W_RWKV, D_MODEL), lambda i: (0, 0)),
            pl.BlockSpec((1, D_MODEL), lambda i: (0, 0)),
        ],
        out_specs=pl.BlockSpec((tm, D_MODEL), lambda i: (i, 0)),
        out_shape=jax.ShapeDtypeStruct((m, D_MODEL), F32),
        compiler_params=pltpu.CompilerParams(
            dimension_semantics=("parallel",), vmem_limit_bytes=VMEM_LIMIT),
        name="outproj",
    )(x2d, oa, ob, w_oa, w_ob, final_g)


def _pad_rows(w, top):
    zeros = jnp.zeros_like(w)
    return jnp.concatenate([w, zeros] if top else [zeros, w], axis=0)


def kernel(x, norm_g, w_in, lambda_q1, lambda_k1, lambda_q2, lambda_k2, subln_g,
           shift_mu, w0, w2, a0, a2, k_k, k_a, r_k, ln_x_g, ln_x_b, w_out, final_g):
    bsz, seq, d = x.shape
    depth = w_in.shape[0]
    assert d == D_MODEL and w_in.shape[2] == D_IN
    m = bsz * seq
    tm = min(512, m)
    tq = min(512, seq)
    assert m % tm == 0 and seq % tq == 0 and seq % CHUNK == 0

    row = lambda v: v.reshape(1, -1).astype(F32)
    h = x.reshape(m, d)
    for l in range(depth):
        proj_a, proj_b = _inproj(h, row(norm_g[l]), w_in[l, :, :A_COLS].astype(BF16),
                                 w_in[l, :, A_COLS:].astype(BF16), tm)
        lam_params = jnp.stack([lambda_q1[l], lambda_k1[l], lambda_q2[l], lambda_k2[l]]).astype(F32)
        o_a = _attention(proj_a.reshape(bsz, seq, A_COLS), lam_params, row(subln_g[l]),
                         bsz, seq, tq, _lambda_init(l))
        o_b = _rwkv(proj_b.reshape(bsz, seq, B_COLS), row(shift_mu[l]), row(w0[l]),
                    _pad_rows(w2[l], True).astype(BF16), row(a0[l]),
                    _pad_rows(a2[l], False).astype(BF16), row(k_k[l]), row(k_a[l]),
                    row(r_k[l]), row(ln_x_g[l]), row(ln_x_b[l]), bsz, seq)
        h = _outproj(h, o_a.reshape(m, W_ATTN), o_b.reshape(m, W_RWKV),
                     w_out[l, :W_ATTN].astype(BF16), w_out[l, W_ATTN:].astype(BF16),
                     row(final_g), tm, final=(l == depth - 1))
    return h.reshape(bsz, seq, d)
```

```python
import functools
import math

import jax
import jax.numpy as jnp
from jax import lax
from jax.experimental import pallas as pl
from jax.experimental.pallas import tpu as pltpu

F32 = jnp.float32
BF16 = jnp.bfloat16

D_MODEL = 1024
CHUNK = 64
W_ATTN = D_MODEL // 2
W_RWKV = D_MODEL - W_ATTN
ATTN_HEAD_DIM = 64
ATTN_HEADS = W_ATTN // (2 * ATTN_HEAD_DIM)
RWKV_HEAD_DIM = 64
RWKV_HEADS = W_RWKV // RWKV_HEAD_DIM
LORA = 64
RMS_EPS = 1e-6
SUBLN_EPS = 1e-5
GN_EPS = 1e-5 * RWKV_HEAD_DIM
A_COLS = 4 * W_ATTN
SHIFT_WIDTH = 3 * W_RWKV + 2 * LORA
B_COLS = SHIFT_WIDTH + W_RWKV
D_IN = A_COLS + B_COLS

LANES = 128
PAIRS = W_RWKV // LANES
VMEM_LIMIT = 56 * 1024 * 1024


def _lambda_init(layer_idx):
    return 0.8 - 0.6 * math.exp(-0.3 * layer_idx)


def _dot(a, b):
    return jnp.dot(a, b, preferred_element_type=F32)


def _dot_nt(a, b):
    return lax.dot_general(a, b, (((1,), (1,)), ((), ())), preferred_element_type=F32)


def _inproj_kernel(x_ref, g_ref, wa_ref, wb_ref, oa_ref, ob_ref):
    x = x_ref[...]
    ms = jnp.mean(x * x, axis=-1, keepdims=True)
    xn = (x * lax.rsqrt(ms + RMS_EPS) * g_ref[...]).astype(BF16)
    q_scale = ATTN_HEAD_DIM ** -0.5 * math.log2(math.e)
    oa_ref[:, :W_ATTN] = (_dot(xn, wa_ref[:, :W_ATTN]) * q_scale).astype(oa_ref.dtype)
    oa_ref[:, W_ATTN:] = _dot(xn, wa_ref[:, W_ATTN:]).astype(oa_ref.dtype)
    ob_ref[...] = _dot(xn, wb_ref[...]).astype(ob_ref.dtype)


def _inproj(x2d, norm_g, w_a, w_b, tm):
    m = x2d.shape[0]
    return pl.pallas_call(
        _inproj_kernel,
        grid=(m // tm,),
        in_specs=[
            pl.BlockSpec((tm, D_MODEL), lambda i: (i, 0)),
            pl.BlockSpec((1, D_MODEL), lambda i: (0, 0)),
            pl.BlockSpec((D_MODEL, A_COLS), lambda i: (0, 0)),
            pl.BlockSpec((D_MODEL, B_COLS), lambda i: (0, 0)),
        ],
        out_specs=[
            pl.BlockSpec((tm, A_COLS), lambda i: (i, 0)),
            pl.BlockSpec((tm, B_COLS), lambda i: (i, 0)),
        ],
        out_shape=[
            jax.ShapeDtypeStruct((m, A_COLS), BF16),
            jax.ShapeDtypeStruct((m, B_COLS), BF16),
        ],
        compiler_params=pltpu.CompilerParams(
            dimension_semantics=("parallel",), vmem_limit_bytes=VMEM_LIMIT),
        name="inproj",
    )(x2d, norm_g, w_a, w_b)


def _attn_kernel(lam_ref, q_ref, k_ref, v_ref, g_ref, sg_ref, o_ref, s_scr, m_scr, l_scr, acc_scr,
                 *, tq, lam_init):
    i = pl.program_id(2)
    lane = lax.broadcasted_iota(jnp.int32, (1, LANES), 1)
    first = lane < ATTN_HEAD_DIM

    lp = lam_ref[...]
    l1 = jnp.sum(lp[0:1] * lp[1:2], axis=-1, keepdims=True)
    l2 = jnp.sum(lp[2:3] * lp[3:4], axis=-1, keepdims=True)
    lam = jnp.exp(l1) - jnp.exp(l2) + lam_init

    q = q_ref[0]
    zero = jnp.zeros_like(q)
    qs = jnp.concatenate([jnp.where(first, q, zero), jnp.where(first, zero, q)], axis=0)

    nt = tq // LANES

    def tiles(s):
        return [s[:, t * LANES:(t + 1) * LANES] for t in range(nt)]

    def scores(j, masked):
        start = pl.multiple_of(j * tq, tq)
        s = _dot_nt(qs, k_ref[0, pl.ds(start, tq), :])
        if masked:
            rq = lax.broadcasted_iota(jnp.int32, (2 * tq, tq), 0)
            rq = jnp.where(rq >= tq, rq - tq, rq) // CHUNK
            ck = lax.broadcasted_iota(jnp.int32, (2 * tq, tq), 1) // CHUNK
            s = jnp.where(ck <= rq, s, -jnp.inf)
        s_scr[j] = s
        return functools.reduce(jnp.maximum, tiles(s))

    m_scr[...] = scores(i, True)

    def pass1(j, carry):
        m_scr[...] = jnp.maximum(m_scr[...], scores(j, False))
        return carry

    lax.fori_loop(0, i, pass1, 0)
    mb = jnp.broadcast_to(jnp.max(m_scr[...], axis=-1, keepdims=True), (2 * tq, LANES))

    def probs_v(j):
        start = pl.multiple_of(j * tq, tq)
        ps = [jnp.exp2(t - mb) for t in tiles(s_scr[j])]
        pv = _dot(jnp.concatenate([p.astype(BF16) for p in ps], axis=1),
                  v_ref[0, pl.ds(start, tq), :])
        return functools.reduce(jnp.add, ps), pv

    l_scr[...], acc_scr[...] = probs_v(i)

    def pass2(j, carry):
        lsum, pv = probs_v(j)
        l_scr[...] += lsum
        acc_scr[...] += pv
        return carry

    lax.fori_loop(0, i, pass2, 0)
    l = jnp.sum(l_scr[...], axis=-1, keepdims=True)

    o = acc_scr[...] / l
    o = o[:tq] - lam * o[tq:]
    ms = jnp.mean(o * o, axis=-1, keepdims=True)
    o = o * lax.rsqrt(ms + SUBLN_EPS) * sg_ref[...] * (1.0 - lam_init)
    g = g_ref[0].astype(F32)
    o_ref[0] = (o * (g / (1.0 + jnp.exp(-g)))).astype(o_ref.dtype)


def _attention(proj_a, lam_params, subln_g, bsz, seq, tq, lam_init):
    nh = ATTN_HEADS
    kernel = functools.partial(_attn_kernel, tq=tq, lam_init=lam_init)
    return pl.pallas_call(
        kernel,
        grid=(bsz, nh, seq // tq),
        in_specs=[
            pl.BlockSpec((4, ATTN_HEAD_DIM), lambda b, h, i: (0, 0)),
            pl.BlockSpec((1, tq, LANES), lambda b, h, i: (b, i, h)),
            pl.BlockSpec((1, seq, LANES), lambda b, h, i: (b, 0, nh + h)),
            pl.BlockSpec((1, seq, LANES), lambda b, h, i: (b, 0, 2 * nh + h)),
            pl.BlockSpec((1, tq, LANES), lambda b, h, i: (b, i, 3 * nh + h)),
            pl.BlockSpec((1, LANES), lambda b, h, i: (0, 0)),
        ],
        out_specs=pl.BlockSpec((1, tq, LANES), lambda b, h, i: (b, i, h)),
        out_shape=jax.ShapeDtypeStruct((bsz, seq, W_ATTN), BF16),
        scratch_shapes=[pltpu.VMEM((seq // tq, 2 * tq, tq), F32)]
        + [pltpu.VMEM((2 * tq, LANES), F32)] * 3,
        compiler_params=pltpu.CompilerParams(
            dimension_semantics=("parallel", "parallel", "arbitrary"),
            vmem_limit_bytes=VMEM_LIMIT),
        name="diff_attn",
    )(lam_params, proj_a, proj_a, proj_a, proj_a, subln_g)


def _split3(x):
    hi = x.astype(BF16)
    r1 = x - hi.astype(F32)
    mid = r1.astype(BF16)
    lo = (r1 - mid.astype(F32)).astype(BF16)
    return hi, mid, lo


def _rwkv_kernel(pb_ref, mu_ref, w0_ref, w2_ref, a0_ref, a2_ref, kk_ref, ka_ref, rk_ref,
                 lng_ref, lnb_ref, o_ref, prev_ref, state_ref, *, nchunk):
    c = CHUNK
    hd = RWKV_HEAD_DIM
    tt = nchunk * c

    @pl.when(pl.program_id(1) == 0)
    def _():
        prev_ref[...] = jnp.zeros_like(prev_ref)
        state_ref[...] = jnp.zeros_like(state_ref)

    z = pb_ref[0].astype(F32)
    zs = z[:, :SHIFT_WIDTH]
    row = lax.broadcasted_iota(jnp.int32, (tt, 1), 0)
    prev = jnp.where(row == 0, prev_ref[...], pltpu.roll(zs, 1, axis=0))
    prev_ref[...] = zs[tt - 1:tt, :]
    xs = zs + (prev - zs) * mu_ref[...]

    r = xs[:, 0:W_RWKV]
    kb = xs[:, W_RWKV:2 * W_RWKV]
    vb = xs[:, 2 * W_RWKV:3 * W_RWKV]
    lora_in = xs[:, 3 * W_RWKV:SHIFT_WIDTH]
    g = z[:, SHIFT_WIDTH:]

    u = w0_ref[...] + _dot(jnp.tanh(lora_in).astype(BF16), w2_ref[...])
    w_log = -(jnp.maximum(-u, 0.0) + jnp.log(1.0 + jnp.exp(-jnp.abs(u)))) - 0.5
    logw = -jnp.exp(w_log)
    a = 1.0 / (1.0 + jnp.exp(-(a0_ref[...] + _dot(lora_in.astype(BF16), a2_ref[...]))))

    lane4 = lax.broadcasted_iota(jnp.int32, (1, W_RWKV), 1)
    first4 = (lane4 % LANES) < hd

    def head_sum(x):
        outs = []
        for p in range(PAIRS):
            xp = x[:, p * LANES:(p + 1) * LANES]
            fp = first4[:, p * LANES:(p + 1) * LANES]
            s0 = jnp.sum(jnp.where(fp, xp, 0.0), axis=-1, keepdims=True)
            s1 = jnp.sum(jnp.where(fp, 0.0, xp), axis=-1, keepdims=True)
            outs.append(jnp.where(fp, s0, s1))
        return jnp.concatenate(outs, axis=1)

    kk = kb * kk_ref[...]
    kk = kk / jnp.maximum(jnp.sqrt(head_sum(kk * kk)), 1e-12)
    k_mod = kb * (1.0 + (a - 1.0) * ka_ref[...])
    bonus = head_sum(r * k_mod * rk_ref[...]) * vb

    ri = lax.broadcasted_iota(jnp.int32, (tt, tt), 0)
    ci = lax.broadcasted_iota(jnp.int32, (tt, tt), 1)
    tri = ((ci <= ri) & ((ci // c) == (ri // c))).astype(BF16)
    hi, mid, lo = _split3(logw)
    cs = _dot(tri, hi) + _dot(tri, mid) + _dot(tri, lo)
    c_ends = [cs[k * c + c - 1:k * c + c, :] for k in range(nchunk)]
    c_end = jnp.concatenate([jnp.broadcast_to(ce, (c, W_RWKV)) for ce in c_ends], axis=0)

    e_neg = jnp.exp(-cs)
    e_end = jnp.exp(c_end - cs)
    kka = kk * a
    at_all = -kk * jnp.exp(cs - logw)
    bt_all = kka * e_neg
    kt_all = k_mod * e_neg
    rt_all = r * jnp.exp(cs)
    bp_all = kka * e_end
    kp_all = k_mod * e_end
    pc_all = [jnp.exp(ce) for ce in c_ends]

    lane = lax.broadcasted_iota(jnp.int32, (1, LANES), 1)
    first = lane < hd
    first2 = jnp.concatenate([first, first], axis=1)
    rowc = lax.broadcasted_iota(jnp.int32, (c, LANES), 0)
    colc = lax.broadcasted_iota(jnp.int32, (c, LANES), 1) % hd
    strict = colc < rowc
    incl = colc <= rowc
    r2 = lax.broadcasted_iota(jnp.int32, (LANES, LANES), 0)
    c2 = lax.broadcasted_iota(jnp.int32, (LANES, LANES), 1)
    same_head = (r2 // hd) == (c2 // hd)
    eye = r2 == c2

    def lower_blk(s):
        return ((r2 // (2 * s)) == (c2 // (2 * s))) & ((r2 // s) % 2 == 1) & ((c2 // s) % 2 == 0)
    zeros_cl = jnp.zeros((c, LANES), F32)

    def h0(x):
        return jnp.where(first, x, 0.0)

    def h1(x):
        return jnp.where(first, 0.0, x)

    units = [(k, p) for k in range(nchunk) for p in range(PAIRS)]

    def cut(x, k, p):
        return x[k * c:(k + 1) * c, p * LANES:(p + 1) * LANES]

    at = [cut(at_all, k, p) for k, p in units]
    rt = [cut(rt_all, k, p) for k, p in units]
    vm = [cut(vb, k, p) for k, p in units]

    mm = []
    for n, (k, p) in enumerate(units):
        bt, kt = cut(bt_all, k, p), cut(kt_all, k, p)
        ar = jnp.concatenate([at[n], rt[n]], axis=0).astype(BF16)
        rh = jnp.concatenate([h0(bt), h0(kt), h1(kt), h1(bt)], axis=0).astype(BF16)
        mm.append(_dot_nt(ar, rh))
    la0 = [jnp.where(strict, m[:c, :LANES], 0.0) for m in mm]
    la1 = [jnp.where(strict, m[:c, LANES:], 0.0) for m in mm]
    mr0 = [jnp.where(incl, m[c:, :LANES], 0.0) for m in mm]
    mr1 = [jnp.where(incl, m[c:, LANES:], 0.0) for m in mm]

    lakv = [_dot(jnp.where(first, la1[n], la0[n]).astype(BF16),
                 jnp.concatenate([h1(vm[n]), h0(vm[n])], axis=0).astype(BF16))
            for n in range(len(units))]

    lbd = [jnp.concatenate([h0(la0[n]), h1(la1[n])], axis=0) for n in range(len(units))]
    tinv = [jnp.where(eye, 1.0, 0.0) + jnp.where(lower_blk(1), l, 0.0) for l in lbd]
    sz = 2
    while sz < c:
        msk = lower_blk(sz)
        tb = [t.astype(BF16) for t in tinv]
        inner = [_dot(jnp.where(msk, l, 0.0).astype(BF16), t).astype(BF16) for l, t in zip(lbd, tb)]
        tinv = [t + _dot(t16, m) for t, t16, m in zip(tinv, tb, inner)]
        sz *= 2
    tz = []
    for n in range(len(units)):
        zc = jnp.concatenate([at[n], lakv[n]], axis=1)
        x = jnp.concatenate([jnp.where(first2, zc, 0.0), jnp.where(first2, 0.0, zc)], axis=0)
        xs2 = _dot(tinv[n].astype(BF16), x.astype(BF16))
        tz.append(xs2[:c] + xs2[c:])
    w = [t[:, :LANES] for t in tz]
    u0 = [t[:, LANES:] for t in tz]

    gmat, hmat = [], []
    for n, (k, p) in enumerate(units):
        bp, kp = cut(bp_all, k, p), cut(kp_all, k, p)
        lt = jnp.concatenate([bp, kp], axis=0).T.astype(BF16)
        rt2 = jnp.concatenate([jnp.concatenate([w[n], u0[n]], axis=1),
                               jnp.concatenate([zeros_cl, vm[n]], axis=1)], axis=0).astype(BF16)
        gh = _dot(lt, rt2)
        pc = pc_all[k][:, p * LANES:(p + 1) * LANES]
        gmat.append(jnp.where(same_head, gh[:, :LANES], 0.0) + jnp.where(eye, pc, 0.0))
        hmat.append(jnp.where(same_head, gh[:, LANES:], 0.0))

    qh, y0 = [], []
    for n in range(len(units)):
        lq = jnp.concatenate([jnp.where(first, mr0[n], mr1[n]), jnp.where(first, mr1[n], mr0[n])],
                             axis=1).astype(BF16)
        rq = jnp.concatenate([
            jnp.concatenate([h0(w[n]), h0(u0[n])], axis=1),
            jnp.concatenate([h1(w[n]), h1(u0[n])], axis=1),
            jnp.concatenate([zeros_cl, h1(vm[n])], axis=1),
            jnp.concatenate([zeros_cl, h0(vm[n])], axis=1)], axis=0).astype(BF16)
        qy = _dot(lq, rq)
        qh.append(rt[n] + qy[:, :LANES])
        y0.append(qy[:, LANES:])

    state = [state_ref[p] for p in range(PAIRS)]
    ys = []
    for n, (k, p) in enumerate(units):
        gs = _dot(jnp.concatenate([gmat[n], qh[n]], axis=0).astype(BF16), state[p].astype(BF16))
        state[p] = gs[:LANES] + hmat[n]
        ys.append(gs[LANES:] + y0[n])
    for p in range(PAIRS):
        state_ref[p] = state[p]

    def pair_mean(t):
        s0 = jnp.sum(h0(t), axis=-1, keepdims=True)
        s1 = jnp.sum(h1(t), axis=-1, keepdims=True)
        return jnp.where(first, s0, s1) * (1.0 / hd)

    for n, (k, p) in enumerate(units):
        sl = slice(p * LANES, (p + 1) * LANES)
        y = ys[n]
        mu = pair_mean(y)
        d = y - mu
        var = pair_mean(d * d)
        yn = d * lax.rsqrt(var + GN_EPS) * lng_ref[:, sl] + lnb_ref[:, sl]
        gp = cut(g, k, p)
        o_ref[0, k * c:(k + 1) * c, sl] = (
            (yn + cut(bonus, k, p)) * (gp / (1.0 + jnp.exp(-gp)))).astype(o_ref.dtype)


def _rwkv(proj_b, mu, w0, w2p, a0, a2p, k_k, k_a, r_k, ln_g, ln_b, bsz, seq, nchunk):
    tt = nchunk * CHUNK
    vec = lambda n: pl.BlockSpec((1, n), lambda b, t: (0, 0))
    mat = pl.BlockSpec((LANES, W_RWKV), lambda b, t: (0, 0))
    return pl.pallas_call(
        functools.partial(_rwkv_kernel, nchunk=nchunk),
        grid=(bsz, seq // tt),
        in_specs=[
            pl.BlockSpec((1, tt, B_COLS), lambda b, t: (b, t, 0)),
            vec(SHIFT_WIDTH), vec(W_RWKV), mat, vec(W_RWKV), mat,
            vec(W_RWKV), vec(W_RWKV), vec(W_RWKV), vec(W_RWKV), vec(W_RWKV),
        ],
        out_specs=pl.BlockSpec((1, tt, W_RWKV), lambda b, t: (b, t, 0)),
        out_shape=jax.ShapeDtypeStruct((bsz, seq, W_RWKV), BF16),
        scratch_shapes=[
            pltpu.VMEM((1, SHIFT_WIDTH), F32),
            pltpu.VMEM((PAIRS, LANES, LANES), F32),
        ],
        compiler_params=pltpu.CompilerParams(
            dimension_semantics=("parallel", "arbitrary"), vmem_limit_bytes=VMEM_LIMIT),
        name="rwkv7",
    )(proj_b, mu, w0, w2p, a0, a2p, k_k, k_a, r_k, ln_g, ln_b)


def _outproj_kernel(x_ref, oa_ref, ob_ref, wa_ref, wb_ref, fg_ref, o_ref, *, final):
    h = x_ref[...] + _dot(oa_ref[...], wa_ref[...]) + _dot(ob_ref[...], wb_ref[...])
    if final:
        ms = jnp.mean(h * h, axis=-1, keepdims=True)
        h = h * lax.rsqrt(ms + RMS_EPS) * fg_ref[...]
    o_ref[...] = h


def _outproj(x2d, oa, ob, w_oa, w_ob, final_g, tm, final):
    m = x2d.shape[0]
    return pl.pallas_call(
        functools.partial(_outproj_kernel, final=final),
        grid=(m // tm,),
        in_specs=[
            pl.BlockSpec((tm, D_MODEL), lambda i: (i, 0)),
            pl.BlockSpec((tm, W_ATTN), lambda i: (i, 0)),
            pl.BlockSpec((tm, W_RWKV), lambda i: (i, 0)),
            pl.BlockSpec((W_ATTN, D_MODEL), lambda i: (0, 0)),
            pl.BlockSpec((W_RWKV, D_MODEL), lambda i: (0, 0)),
            pl.BlockSpec((1, D_MODEL), lambda i: (0, 0)),
        ],
        out_specs=pl.BlockSpec((tm, D_MODEL), lambda i: (i, 0)),
        out_shape=jax.ShapeDtypeStruct((m, D_MODEL), F32),
        compiler_params=pltpu.CompilerParams(
            dimension_semantics=("parallel",), vmem_limit_bytes=VMEM_LIMIT),
        name="outproj",
    )(x2d, oa, ob, w_oa, w_ob, final_g)


def _pad_rows(w, top):
    zeros = jnp.zeros_like(w)
    return jnp.concatenate([w, zeros] if top else [zeros, w], axis=0)


def kernel(x, norm_g, w_in, lambda_q1, lambda_k1, lambda_q2, lambda_k2, subln_g,
           shift_mu, w0, w2, a0, a2, k_k, k_a, r_k, ln_x_g, ln_x_b, w_out, final_g):
    bsz, seq, d = x.shape
    depth = w_in.shape[0]
    assert d == D_MODEL and w_in.shape[2] == D_IN
    m = bsz * seq
    tm = min(512, m)
    tq = min(512, seq)
    nchunk = 2
    assert m % tm == 0 and seq % tq == 0 and seq % (nchunk * CHUNK) == 0

    row = lambda v: v.reshape(1, -1).astype(F32)
    h = x.reshape(m, d)
    for l in range(depth):
        proj_a, proj_b = _inproj(h, row(norm_g[l]), w_in[l, :, :A_COLS].astype(BF16),
                                 w_in[l, :, A_COLS:].astype(BF16), tm)
        lam_params = jnp.stack([lambda_q1[l], lambda_k1[l], lambda_q2[l], lambda_k2[l]]).astype(F32)
        o_a = _attention(proj_a.reshape(bsz, seq, A_COLS), lam_params, row(subln_g[l]),
                         bsz, seq, tq, _lambda_init(l))
        o_b = _rwkv(proj_b.reshape(bsz, seq, B_COLS), row(shift_mu[l]), row(w0[l]),
                    _pad_rows(w2[l], True).astype(BF16), row(a0[l]),
                    _pad_rows(a2[l], False).astype(BF16), row(k_k[l]), row(k_a[l]),
                    row(r_k[l]), row(ln_x_g[l]), row(ln_x_b[l]), bsz, seq, nchunk)
        h = _outproj(h, o_a.reshape(m, W_ATTN), o_b.reshape(m, W_RWKV),
                     w_out[l, :W_ATTN].astype(BF16), w_out[l, W_ATTN:].astype(BF16),
                     row(final_g), tm, final=(l == depth - 1))
    return h.reshape(bsz, seq, d)
```

```python
import functools
import math

import jax
import jax.numpy as jnp
from jax import lax
from jax.experimental import pallas as pl
from jax.experimental.pallas import tpu as pltpu

F32 = jnp.float32
BF16 = jnp.bfloat16

D_MODEL = 1024
CHUNK = 64
W_ATTN = D_MODEL // 2
W_RWKV = D_MODEL - W_ATTN
ATTN_HEAD_DIM = 64
ATTN_HEADS = W_ATTN // (2 * ATTN_HEAD_DIM)
RWKV_HEAD_DIM = 64
RWKV_HEADS = W_RWKV // RWKV_HEAD_DIM
LORA = 64
RMS_EPS = 1e-6
SUBLN_EPS = 1e-5
GN_EPS = 1e-5 * RWKV_HEAD_DIM
A_COLS = 4 * W_ATTN
SHIFT_WIDTH = 3 * W_RWKV + 2 * LORA
B_COLS = SHIFT_WIDTH + W_RWKV
D_IN = A_COLS + B_COLS

LANES = 128
PAIRS = W_RWKV // LANES
VMEM_LIMIT = 56 * 1024 * 1024


def _lambda_init(layer_idx):
    return 0.8 - 0.6 * math.exp(-0.3 * layer_idx)


def _dot(a, b):
    return jnp.dot(a, b, preferred_element_type=F32)


def _dot_nt(a, b):
    return lax.dot_general(a, b, (((1,), (1,)), ((), ())), preferred_element_type=F32)


def _inproj_kernel(x_ref, g_ref, wa_ref, wb_ref, oa_ref, ob_ref):
    x = x_ref[...]
    ms = jnp.mean(x * x, axis=-1, keepdims=True)
    xn = (x * lax.rsqrt(ms + RMS_EPS) * g_ref[...]).astype(BF16)
    q_scale = ATTN_HEAD_DIM ** -0.5 * math.log2(math.e)
    oa_ref[:, :W_ATTN] = (_dot(xn, wa_ref[:, :W_ATTN]) * q_scale).astype(oa_ref.dtype)
    oa_ref[:, W_ATTN:] = _dot(xn, wa_ref[:, W_ATTN:]).astype(oa_ref.dtype)
    ob_ref[...] = _dot(xn, wb_ref[...]).astype(ob_ref.dtype)


def _inproj(x2d, norm_g, w_a, w_b, tm):
    m = x2d.shape[0]
    return pl.pallas_call(
        _inproj_kernel,
        grid=(m // tm,),
        in_specs=[
            pl.BlockSpec((tm, D_MODEL), lambda i: (i, 0)),
            pl.BlockSpec((1, D_MODEL), lambda i: (0, 0)),
            pl.BlockSpec((D_MODEL, A_COLS), lambda i: (0, 0)),
            pl.BlockSpec((D_MODEL, B_COLS), lambda i: (0, 0)),
        ],
        out_specs=[
            pl.BlockSpec((tm, A_COLS), lambda i: (i, 0)),
            pl.BlockSpec((tm, B_COLS), lambda i: (i, 0)),
        ],
        out_shape=[
            jax.ShapeDtypeStruct((m, A_COLS), BF16),
            jax.ShapeDtypeStruct((m, B_COLS), BF16),
        ],
        compiler_params=pltpu.CompilerParams(
            dimension_semantics=("parallel",), vmem_limit_bytes=VMEM_LIMIT),
        name="inproj",
    )(x2d, norm_g, w_a, w_b)


ATTN_GROUP = 2


def _attn_kernel(lam_ref, q_ref, k_ref, v_ref, g_ref, sg_ref, o_ref, s_scr, m_scr, l_scr, acc_scr,
                 *, tq, lam_init):
    i = pl.program_id(2)
    lane = lax.broadcasted_iota(jnp.int32, (1, LANES), 1)
    first = lane < ATTN_HEAD_DIM
    heads = range(ATTN_GROUP)

    lp = lam_ref[...]
    l1 = jnp.sum(lp[0:1] * lp[1:2], axis=-1, keepdims=True)
    l2 = jnp.sum(lp[2:3] * lp[3:4], axis=-1, keepdims=True)
    lam = jnp.exp(l1) - jnp.exp(l2) + lam_init

    def head(ref, hh, rows=slice(None)):
        return ref[0, rows, hh * LANES:(hh + 1) * LANES]

    qs = []
    for hh in heads:
        q = head(q_ref, hh)
        zero = jnp.zeros_like(q)
        qs.append(jnp.concatenate([jnp.where(first, q, zero), jnp.where(first, zero, q)], axis=0))

    nt = tq // LANES

    def tiles(s):
        return [s[:, t * LANES:(t + 1) * LANES] for t in range(nt)]

    def scores(hh, j, masked):
        start = pl.multiple_of(j * tq, tq)
        s = _dot_nt(qs[hh], head(k_ref, hh, pl.ds(start, tq)))
        if masked:
            rq = lax.broadcasted_iota(jnp.int32, (2 * tq, tq), 0)
            rq = jnp.where(rq >= tq, rq - tq, rq) // CHUNK
            ck = lax.broadcasted_iota(jnp.int32, (2 * tq, tq), 1) // CHUNK
            s = jnp.where(ck <= rq, s, -jnp.inf)
        s_scr[hh, j] = s
        return functools.reduce(jnp.maximum, tiles(s))

    for hh in heads:
        m_scr[hh] = scores(hh, i, True)

    def pass1(j, carry):
        for hh in heads:
            m_scr[hh] = jnp.maximum(m_scr[hh], scores(hh, j, False))
        return carry

    lax.fori_loop(0, i, pass1, 0)
    mb = [jnp.broadcast_to(jnp.max(m_scr[hh], axis=-1, keepdims=True), (2 * tq, LANES))
          for hh in heads]

    def probs_v(hh, j):
        start = pl.multiple_of(j * tq, tq)
        ps = [jnp.exp2(t - mb[hh]) for t in tiles(s_scr[hh, j])]
        pv = _dot(jnp.concatenate([p.astype(BF16) for p in ps], axis=1),
                  head(v_ref, hh, pl.ds(start, tq)))
        return functools.reduce(jnp.add, ps), pv

    for hh in heads:
        l_scr[hh], acc_scr[hh] = probs_v(hh, i)

    def pass2(j, carry):
        for hh in heads:
            lsum, pv = probs_v(hh, j)
            l_scr[hh] += lsum
            acc_scr[hh] += pv
        return carry

    lax.fori_loop(0, i, pass2, 0)

    for hh in heads:
        l = jnp.sum(l_scr[hh], axis=-1, keepdims=True)
        o = acc_scr[hh] / l
        o = o[:tq] - lam * o[tq:]
        ms = jnp.mean(o * o, axis=-1, keepdims=True)
        o = o * lax.rsqrt(ms + SUBLN_EPS) * sg_ref[...] * (1.0 - lam_init)
        g = head(g_ref, hh).astype(F32)
        o_ref[0, :, hh * LANES:(hh + 1) * LANES] = (
            o * (g / (1.0 + jnp.exp(-g)))).astype(o_ref.dtype)


def _attention(proj_a, lam_params, subln_g, bsz, seq, tq, lam_init):
    ng = ATTN_HEADS // ATTN_GROUP
    gw = ATTN_GROUP * LANES
    kernel = functools.partial(_attn_kernel, tq=tq, lam_init=lam_init)
    return pl.pallas_call(
        kernel,
        grid=(bsz, ng, seq // tq),
        in_specs=[
            pl.BlockSpec((4, ATTN_HEAD_DIM), lambda b, h, i: (0, 0)),
            pl.BlockSpec((1, tq, gw), lambda b, h, i: (b, i, h)),
            pl.BlockSpec((1, seq, gw), lambda b, h, i: (b, 0, ng + h)),
            pl.BlockSpec((1, seq, gw), lambda b, h, i: (b, 0, 2 * ng + h)),
            pl.BlockSpec((1, tq, gw), lambda b, h, i: (b, i, 3 * ng + h)),
            pl.BlockSpec((1, LANES), lambda b, h, i: (0, 0)),
        ],
        out_specs=pl.BlockSpec((1, tq, gw), lambda b, h, i: (b, i, h)),
        out_shape=jax.ShapeDtypeStruct((bsz, seq, W_ATTN), BF16),
        scratch_shapes=[pltpu.VMEM((ATTN_GROUP, seq // tq, 2 * tq, tq), F32)]
        + [pltpu.VMEM((ATTN_GROUP, 2 * tq, LANES), F32)] * 3,
        compiler_params=pltpu.CompilerParams(
            dimension_semantics=("parallel", "parallel", "arbitrary"),
            vmem_limit_bytes=VMEM_LIMIT),
        name="diff_attn",
    )(lam_params, proj_a, proj_a, proj_a, proj_a, subln_g)


def _split3(x):
    hi = x.astype(BF16)
    r1 = x - hi.astype(F32)
    mid = r1.astype(BF16)
    lo = (r1 - mid.astype(F32)).astype(BF16)
    return hi, mid, lo


def _rwkv_kernel(pb_ref, mu_ref, w0_ref, w2_ref, a0_ref, a2_ref, kk_ref, ka_ref, rk_ref,
                 lng_ref, lnb_ref, o_ref, prev_ref, state_ref, *, nchunk):
    c = CHUNK
    hd = RWKV_HEAD_DIM
    tt = nchunk * c

    @pl.when(pl.program_id(1) == 0)
    def _():
        prev_ref[...] = jnp.zeros_like(prev_ref)
        state_ref[...] = jnp.zeros_like(state_ref)

    z = pb_ref[0].astype(F32)
    zs = z[:, :SHIFT_WIDTH]
    row = lax.broadcasted_iota(jnp.int32, (tt, 1), 0)
    prev = jnp.where(row == 0, prev_ref[...], pltpu.roll(zs, 1, axis=0))
    prev_ref[...] = zs[tt - 1:tt, :]
    xs = zs + (prev - zs) * mu_ref[...]

    r = xs[:, 0:W_RWKV]
    kb = xs[:, W_RWKV:2 * W_RWKV]
    vb = xs[:, 2 * W_RWKV:3 * W_RWKV]
    lora_in = xs[:, 3 * W_RWKV:SHIFT_WIDTH]
    g = z[:, SHIFT_WIDTH:]

    u = w0_ref[...] + _dot(jnp.tanh(lora_in).astype(BF16), w2_ref[...])
    w_log = -(jnp.maximum(-u, 0.0) + jnp.log(1.0 + jnp.exp(-jnp.abs(u)))) - 0.5
    logw = -jnp.exp(w_log)
    a = 1.0 / (1.0 + jnp.exp(-(a0_ref[...] + _dot(lora_in.astype(BF16), a2_ref[...]))))

    lane4 = lax.broadcasted_iota(jnp.int32, (1, W_RWKV), 1)
    first4 = (lane4 % LANES) < hd

    def head_sum(x):
        outs = []
        for p in range(PAIRS):
            xp = x[:, p * LANES:(p + 1) * LANES]
            fp = first4[:, p * LANES:(p + 1) * LANES]
            s0 = jnp.sum(jnp.where(fp, xp, 0.0), axis=-1, keepdims=True)
            s1 = jnp.sum(jnp.where(fp, 0.0, xp), axis=-1, keepdims=True)
            outs.append(jnp.where(fp, s0, s1))
        return jnp.concatenate(outs, axis=1)

    kk = kb * kk_ref[...]
    kk = kk / jnp.maximum(jnp.sqrt(head_sum(kk * kk)), 1e-12)
    k_mod = kb * (1.0 + (a - 1.0) * ka_ref[...])
    bonus = head_sum(r * k_mod * rk_ref[...]) * vb

    ri = lax.broadcasted_iota(jnp.int32, (tt, tt), 0)
    ci = lax.broadcasted_iota(jnp.int32, (tt, tt), 1)
    tri = ((ci <= ri) & ((ci // c) == (ri // c))).astype(BF16)
    hi, mid, lo = _split3(logw)
    cs = _dot(tri, hi) + _dot(tri, mid) + _dot(tri, lo)
    c_ends = [cs[k * c + c - 1:k * c + c, :] for k in range(nchunk)]
    c_end = jnp.concatenate([jnp.broadcast_to(ce, (c, W_RWKV)) for ce in c_ends], axis=0)

    e_neg = jnp.exp(-cs)
    e_end = jnp.exp(c_end - cs)
    kka = kk * a
    at_all = -kk * jnp.exp(cs - logw)
    bt_all = kka * e_neg
    kt_all = k_mod * e_neg
    rt_all = r * jnp.exp(cs)
    bp_all = kka * e_end
    kp_all = k_mod * e_end
    pc_all = [jnp.exp(ce) for ce in c_ends]

    lane = lax.broadcasted_iota(jnp.int32, (1, LANES), 1)
    first = lane < hd
    first2 = jnp.concatenate([first, first], axis=1)
    rowc = lax.broadcasted_iota(jnp.int32, (c, LANES), 0)
    colc = lax.broadcasted_iota(jnp.int32, (c, LANES), 1) % hd
    strict = colc < rowc
    incl = colc <= rowc
    r2 = lax.broadcasted_iota(jnp.int32, (LANES, LANES), 0)
    c2 = lax.broadcasted_iota(jnp.int32, (LANES, LANES), 1)
    same_head = (r2 // hd) == (c2 // hd)
    eye = r2 == c2

    def lower_blk(s):
        return ((r2 // (2 * s)) == (c2 // (2 * s))) & ((r2 // s) % 2 == 1) & ((c2 // s) % 2 == 0)
    zeros_cl = jnp.zeros((c, LANES), F32)

    def h0(x):
        return jnp.where(first, x, 0.0)

    def h1(x):
        return jnp.where(first, 0.0, x)

    units = [(k, p) for k in range(nchunk) for p in range(PAIRS)]

    def cut(x, k, p):
        return x[k * c:(k + 1) * c, p * LANES:(p + 1) * LANES]

    at = [cut(at_all, k, p) for k, p in units]
    rt = [cut(rt_all, k, p) for k, p in units]
    vm = [cut(vb, k, p) for k, p in units]

    mm = []
    for n, (k, p) in enumerate(units):
        bt, kt = cut(bt_all, k, p), cut(kt_all, k, p)
        ar = jnp.concatenate([at[n], rt[n]], axis=0).astype(BF16)
        rh = jnp.concatenate([h0(bt), h0(kt), h1(kt), h1(bt)], axis=0).astype(BF16)
        mm.append(_dot_nt(ar, rh))
    la0 = [jnp.where(strict, m[:c, :LANES], 0.0) for m in mm]
    la1 = [jnp.where(strict, m[:c, LANES:], 0.0) for m in mm]
    mr0 = [jnp.where(incl, m[c:, :LANES], 0.0) for m in mm]
    mr1 = [jnp.where(incl, m[c:, LANES:], 0.0) for m in mm]

    lakv = [_dot(jnp.where(first, la1[n], la0[n]).astype(BF16),
                 jnp.concatenate([h1(vm[n]), h0(vm[n])], axis=0).astype(BF16))
            for n in range(len(units))]

    lbd = [jnp.concatenate([h0(la0[n]), h1(la1[n])], axis=0) for n in range(len(units))]
    tinv = [(jnp.where(eye, 1.0, 0.0) + jnp.where(lower_blk(1), l, 0.0)).astype(BF16) for l in lbd]
    lbd16 = [l.astype(BF16) for l in lbd]
    sz = 2
    while sz < c:
        msk = lower_blk(sz)
        inner = [_dot(l, t).astype(BF16) for l, t in zip(lbd16, tinv)]
        tinv = [jnp.where(msk, _dot(t, m).astype(BF16), t) for t, m in zip(tinv, inner)]
        sz *= 2
    tz = []
    for n in range(len(units)):
        zc = jnp.concatenate([at[n], lakv[n]], axis=1)
        x = jnp.concatenate([jnp.where(first2, zc, 0.0), jnp.where(first2, 0.0, zc)], axis=0)
        xs2 = _dot(tinv[n], x.astype(BF16))
        tz.append(xs2[:c] + xs2[c:])
    w = [t[:, :LANES] for t in tz]
    u0 = [t[:, LANES:] for t in tz]

    gmat, hmat = [], []
    for n, (k, p) in enumerate(units):
        bp, kp = cut(bp_all, k, p), cut(kp_all, k, p)
        lt = jnp.concatenate([bp, kp], axis=0).T.astype(BF16)
        rt2 = jnp.concatenate([jnp.concatenate([w[n], u0[n]], axis=1),
                               jnp.concatenate([zeros_cl, vm[n]], axis=1)], axis=0).astype(BF16)
        gh = _dot(lt, rt2)
        pc = pc_all[k][:, p * LANES:(p + 1) * LANES]
        gmat.append(jnp.where(same_head, gh[:, :LANES], 0.0) + jnp.where(eye, pc, 0.0))
        hmat.append(jnp.where(same_head, gh[:, LANES:], 0.0))

    qh, y0 = [], []
    for n in range(len(units)):
        lq = jnp.concatenate([jnp.where(first, mr0[n], mr1[n]), jnp.where(first, mr1[n], mr0[n])],
                             axis=1).astype(BF16)
        rq = jnp.concatenate([
            jnp.concatenate([h0(w[n]), h0(u0[n])], axis=1),
            jnp.concatenate([h1(w[n]), h1(u0[n])], axis=1),
            jnp.concatenate([zeros_cl, h1(vm[n])], axis=1),
            jnp.concatenate([zeros_cl, h0(vm[n])], axis=1)], axis=0).astype(BF16)
        qy = _dot(lq, rq)
        qh.append(rt[n] + qy[:, :LANES])
        y0.append(qy[:, LANES:])

    state = [state_ref[p] for p in range(PAIRS)]
    ys = []
    for n, (k, p) in enumerate(units):
        gs = _dot(jnp.concatenate([gmat[n], qh[n]], axis=0).astype(BF16), state[p].astype(BF16))
        state[p] = gs[:LANES] + hmat[n]
        ys.append(gs[LANES:] + y0[n])
    for p in range(PAIRS):
        state_ref[p] = state[p]

    def pair_mean(t):
        s0 = jnp.sum(h0(t), axis=-1, keepdims=True)
        s1 = jnp.sum(h1(t), axis=-1, keepdims=True)
        return jnp.where(first, s0, s1) * (1.0 / hd)

    for n, (k, p) in enumerate(units):
        sl = slice(p * LANES, (p + 1) * LANES)
        y = ys[n]
        mu = pair_mean(y)
        d = y - mu
        var = pair_mean(d * d)
        yn = d * lax.rsqrt(var + GN_EPS) * lng_ref[:, sl] + lnb_ref[:, sl]
        gp = cut(g, k, p)
        o_ref[0, k * c:(k + 1) * c, sl] = (
            (yn + cut(bonus, k, p)) * (gp / (1.0 + jnp.exp(-gp)))).astype(o_ref.dtype)


def _rwkv(proj_b, mu, w0, w2p, a0, a2p, k_k, k_a, r_k, ln_g, ln_b, bsz, seq, nchunk):
    tt = nchunk * CHUNK
    vec = lambda n: pl.BlockSpec((1, n), lambda b, t: (0, 0))
    mat = pl.BlockSpec((LANES, W_RWKV), lambda b, t: (0, 0))
    return pl.pallas_call(
        functools.partial(_rwkv_kernel, nchunk=nchunk),
        grid=(bsz, seq // tt),
        in_specs=[
            pl.BlockSpec((1, tt, B_COLS), lambda b, t: (b, t, 0)),
            vec(SHIFT_WIDTH), vec(W_RWKV), mat, vec(W_RWKV), mat,
            vec(W_RWKV), vec(W_RWKV), vec(W_RWKV), vec(W_RWKV), vec(W_RWKV),
        ],
        out_specs=pl.BlockSpec((1, tt, W_RWKV), lambda b, t: (b, t, 0)),
        out_shape=jax.ShapeDtypeStruct((bsz, seq, W_RWKV), BF16),
        scratch_shapes=[
            pltpu.VMEM((1, SHIFT_WIDTH), F32),
            pltpu.VMEM((PAIRS, LANES, LANES), F32),
        ],
        compiler_params=pltpu.CompilerParams(
            dimension_semantics=("parallel", "arbitrary"), vmem_limit_bytes=VMEM_LIMIT),
        name="rwkv7",
    )(proj_b, mu, w0, w2p, a0, a2p, k_k, k_a, r_k, ln_g, ln_b)


def _outproj_kernel(x_ref, oa_ref, ob_ref, wa_ref, wb_ref, fg_ref, o_ref, *, final):
    h = x_ref[...] + _dot(oa_ref[...], wa_ref[...]) + _dot(ob_ref[...], wb_ref[...])
    if final:
        ms = jnp.mean(h * h, axis=-1, keepdims=True)
        h = h * lax.rsqrt(ms + RMS_EPS) * fg_ref[...]
    o_ref[...] = h


def _outproj(x2d, oa, ob, w_oa, w_ob, final_g, tm, final):
    m = x2d.shape[0]
    return pl.pallas_call(
        functools.partial(_outproj_kernel, final=final),
        grid=(m // tm,),
        in_specs=[
            pl.BlockSpec((tm, D_MODEL), lambda i: (i, 0)),
            pl.BlockSpec((tm, W_ATTN), lambda i: (i, 0)),
            pl.BlockSpec((tm, W_RWKV), lambda i: (i, 0)),
            pl.BlockSpec((W_ATTN, D_MODEL), lambda i: (0, 0)),
            pl.BlockSpec((W_RWKV, D_MODEL), lambda i: (0, 0)),
            pl.BlockSpec((1, D_MODEL), lambda i: (0, 0)),
        ],
        out_specs=pl.BlockSpec((tm, D_MODEL), lambda i: (i, 0)),
        out_shape=jax.ShapeDtypeStruct((m, D_MODEL), F32),
        compiler_params=pltpu.CompilerParams(
            dimension_semantics=("parallel",), vmem_limit_bytes=VMEM_LIMIT),
        name="outproj",
    )(x2d, oa, ob, w_oa, w_ob, final_g)


def _pad_rows(w, top):
    zeros = jnp.zeros_like(w)
    return jnp.concatenate([w, zeros] if top else [zeros, w], axis=0)


def kernel(x, norm_g, w_in, lambda_q1, lambda_k1, lambda_q2, lambda_k2, subln_g,
           shift_mu, w0, w2, a0, a2, k_k, k_a, r_k, ln_x_g, ln_x_b, w_out, final_g):
    bsz, seq, d = x.shape
    depth = w_in.shape[0]
    assert d == D_MODEL and w_in.shape[2] == D_IN
    m = bsz * seq
    tm = min(512, m)
    tq = min(512, seq)
    nchunk = 4
    assert m % tm == 0 and seq % tq == 0 and seq % (nchunk * CHUNK) == 0

    row = lambda v: v.reshape(1, -1).astype(F32)
    h = x.reshape(m, d)
    for l in range(depth):
        proj_a, proj_b = _inproj(h, row(norm_g[l]), w_in[l, :, :A_COLS].astype(BF16),
                                 w_in[l, :, A_COLS:].astype(BF16), tm)
        lam_params = jnp.stack([lambda_q1[l], lambda_k1[l], lambda_q2[l], lambda_k2[l]]).astype(F32)
        o_a = _attention(proj_a.reshape(bsz, seq, A_COLS), lam_params, row(subln_g[l]),
                         bsz, seq, tq, _lambda_init(l))
        o_b = _rwkv(proj_b.reshape(bsz, seq, B_COLS), row(shift_mu[l]), row(w0[l]),
                    _pad_rows(w2[l], True).astype(BF16), row(a0[l]),
                    _pad_rows(a2[l], False).astype(BF16), row(k_k[l]), row(k_a[l]),
                    row(r_k[l]), row(ln_x_g[l]), row(ln_x_b[l]), bsz, seq, nchunk)
        h = _outproj(h, o_a.reshape(m, W_ATTN), o_b.reshape(m, W_RWKV),
                     w_out[l, :W_ATTN].astype(BF16), w_out[l, W_ATTN:].astype(BF16),
                     row(final_g), tm, final=(l == depth - 1))
    return h.reshape(bsz, seq, d)
```

```python
import functools
import math

import jax
import jax.numpy as jnp
from jax import lax
from jax.experimental import pallas as pl
from jax.experimental.pallas import tpu as pltpu

F32 = jnp.float32
BF16 = jnp.bfloat16

D_MODEL = 1024
CHUNK = 64
W_ATTN = D_MODEL // 2
W_RWKV = D_MODEL - W_ATTN
ATTN_HEAD_DIM = 64
ATTN_HEADS = W_ATTN // (2 * ATTN_HEAD_DIM)
RWKV_HEAD_DIM = 64
RWKV_HEADS = W_RWKV // RWKV_HEAD_DIM
LORA = 64
RMS_EPS = 1e-6
SUBLN_EPS = 1e-5
GN_EPS = 1e-5 * RWKV_HEAD_DIM
A_COLS = 4 * W_ATTN
SHIFT_WIDTH = 3 * W_RWKV + 2 * LORA
B_COLS = SHIFT_WIDTH + W_RWKV
D_IN = A_COLS + B_COLS

LANES = 128
PAIRS = W_RWKV // LANES
VMEM_LIMIT = 56 * 1024 * 1024


def _lambda_init(layer_idx):
    return 0.8 - 0.6 * math.exp(-0.3 * layer_idx)


def _dot(a, b):
    return jnp.dot(a, b, preferred_element_type=F32)


def _dot_nt(a, b):
    return lax.dot_general(a, b, (((1,), (1,)), ((), ())), preferred_element_type=F32)


def _inproj_kernel(x_ref, g_ref, wa_ref, wb_ref, oa_ref, ob_ref):
    x = x_ref[...]
    ms = jnp.mean(x * x, axis=-1, keepdims=True)
    xn = (x * lax.rsqrt(ms + RMS_EPS) * g_ref[...]).astype(BF16)
    q_scale = ATTN_HEAD_DIM ** -0.5 * math.log2(math.e)
    oa_ref[:, :W_ATTN] = (_dot(xn, wa_ref[:, :W_ATTN]) * q_scale).astype(oa_ref.dtype)
    oa_ref[:, W_ATTN:] = _dot(xn, wa_ref[:, W_ATTN:]).astype(oa_ref.dtype)
    ob_ref[...] = _dot(xn, wb_ref[...]).astype(ob_ref.dtype)


def _inproj(x2d, norm_g, w_a, w_b, tm):
    m = x2d.shape[0]
    return pl.pallas_call(
        _inproj_kernel,
        grid=(m // tm,),
        in_specs=[
            pl.BlockSpec((tm, D_MODEL), lambda i: (i, 0)),
            pl.BlockSpec((1, D_MODEL), lambda i: (0, 0)),
            pl.BlockSpec((D_MODEL, A_COLS), lambda i: (0, 0)),
            pl.BlockSpec((D_MODEL, B_COLS), lambda i: (0, 0)),
        ],
        out_specs=[
            pl.BlockSpec((tm, A_COLS), lambda i: (i, 0)),
            pl.BlockSpec((tm, B_COLS), lambda i: (i, 0)),
        ],
        out_shape=[
            jax.ShapeDtypeStruct((m, A_COLS), BF16),
            jax.ShapeDtypeStruct((m, B_COLS), BF16),
        ],
        compiler_params=pltpu.CompilerParams(
            dimension_semantics=("parallel",), vmem_limit_bytes=VMEM_LIMIT),
        name="inproj",
    )(x2d, norm_g, w_a, w_b)


ATTN_GROUP = 2


def _attn_kernel(lam_ref, q_ref, k_ref, v_ref, g_ref, sg_ref, o_ref, s_scr, m_scr, acc_scr,
                 *, tq, lam_init):
    i = pl.program_id(2)
    lane = lax.broadcasted_iota(jnp.int32, (1, LANES), 1)
    first = lane < ATTN_HEAD_DIM
    heads = range(ATTN_GROUP)

    lp = lam_ref[...]
    l1 = jnp.sum(lp[0:1] * lp[1:2], axis=-1, keepdims=True)
    l2 = jnp.sum(lp[2:3] * lp[3:4], axis=-1, keepdims=True)
    lam = jnp.exp(l1) - jnp.exp(l2) + lam_init

    def head(ref, hh, rows=slice(None)):
        return ref[0, rows, hh * LANES:(hh + 1) * LANES]

    qs = []
    for hh in heads:
        q = head(q_ref, hh)
        zero = jnp.zeros_like(q)
        qs.append(jnp.concatenate([jnp.where(first, q, zero), jnp.where(first, zero, q)], axis=0))

    nt = tq // LANES

    def tiles(s):
        return [s[:, t * LANES:(t + 1) * LANES] for t in range(nt)]

    def scores(hh, j):
        start = pl.multiple_of(j * tq, tq)
        s = _dot_nt(qs[hh], head(k_ref, hh, pl.ds(start, tq)))
        s_scr[hh, j] = s
        return functools.reduce(jnp.maximum, tiles(s))

    diag = pl.multiple_of(i * tq, tq)
    sub_row = lax.broadcasted_iota(jnp.int32, (2 * LANES, 1), 0) % LANES
    hidden = (sub_row < CHUNK) & (lane >= CHUNK)

    def sub_rows(u):
        return slice(u * LANES, (u + 1) * LANES), slice(tq + u * LANES, tq + (u + 1) * LANES)

    def diag_scores(hh):
        for u in range(nt):
            r1, r2 = sub_rows(u)
            nk = (u + 1) * LANES
            s = _dot_nt(jnp.concatenate([qs[hh][r1], qs[hh][r2]], axis=0),
                        head(k_ref, hh, pl.ds(diag, nk)))
            parts = [s[:, t * LANES:(t + 1) * LANES] for t in range(u)]
            parts.append(jnp.where(hidden, -jnp.inf, s[:, u * LANES:nk]))
            for t, part in enumerate(parts):
                s_scr[hh, i, r1, t * LANES:(t + 1) * LANES] = part[:LANES]
                s_scr[hh, i, r2, t * LANES:(t + 1) * LANES] = part[LANES:]
            m = functools.reduce(jnp.maximum, parts)
            m_scr[hh, r1] = m[:LANES]
            m_scr[hh, r2] = m[LANES:]

    for hh in heads:
        diag_scores(hh)

    def pass1(j, carry):
        for hh in heads:
            m_scr[hh] = jnp.maximum(m_scr[hh], scores(hh, j))
        return carry

    lax.fori_loop(0, i, pass1, 0)
    mb = [jnp.broadcast_to(jnp.max(m_scr[hh], axis=-1, keepdims=True), (2 * tq, LANES))
          for hh in heads]

    def with_ones(v):
        return jnp.concatenate([v, jnp.ones_like(v)], axis=1)

    def probs_v(hh, j):
        start = pl.multiple_of(j * tq, tq)
        ps = [jnp.exp2(t - mb[hh]).astype(BF16) for t in tiles(s_scr[hh, j])]
        return _dot(jnp.concatenate(ps, axis=1), with_ones(head(v_ref, hh, pl.ds(start, tq))))

    def diag_probs_v(hh):
        for u in range(nt):
            r1, r2 = sub_rows(u)
            nk = (u + 1) * LANES
            mbu = jnp.concatenate([mb[hh][r1], mb[hh][r2]], axis=0)
            ps = [jnp.exp2(jnp.concatenate([s_scr[hh, i, r1, t * LANES:(t + 1) * LANES],
                                            s_scr[hh, i, r2, t * LANES:(t + 1) * LANES]], axis=0)
                           - mbu).astype(BF16) for t in range(u + 1)]
            pv = _dot(jnp.concatenate(ps, axis=1),
                      with_ones(head(v_ref, hh, pl.ds(diag, nk))))
            acc_scr[hh, r1], acc_scr[hh, r2] = pv[:LANES], pv[LANES:]

    for hh in heads:
        diag_probs_v(hh)

    def pass2(j, carry):
        for hh in heads:
            acc_scr[hh] += probs_v(hh, j)
        return carry

    lax.fori_loop(0, i, pass2, 0)

    for hh in heads:
        acc = acc_scr[hh]
        o = acc[:, :LANES] / acc[:, LANES:]
        o = o[:tq] - lam * o[tq:]
        ms = jnp.mean(o * o, axis=-1, keepdims=True)
        o = o * lax.rsqrt(ms + SUBLN_EPS) * sg_ref[...] * (1.0 - lam_init)
        g = head(g_ref, hh).astype(F32)
        o_ref[0, :, hh * LANES:(hh + 1) * LANES] = (
            o * (g / (1.0 + jnp.exp(-g)))).astype(o_ref.dtype)


def _attention(proj_a, lam_params, subln_g, bsz, seq, tq, lam_init):
    ng = ATTN_HEADS // ATTN_GROUP
    gw = ATTN_GROUP * LANES
    kernel = functools.partial(_attn_kernel, tq=tq, lam_init=lam_init)
    return pl.pallas_call(
        kernel,
        grid=(bsz, ng, seq // tq),
        in_specs=[
            pl.BlockSpec((4, ATTN_HEAD_DIM), lambda b, h, i: (0, 0)),
            pl.BlockSpec((1, tq, gw), lambda b, h, i: (b, i, h)),
            pl.BlockSpec((1, seq, gw), lambda b, h, i: (b, 0, ng + h)),
            pl.BlockSpec((1, seq, gw), lambda b, h, i: (b, 0, 2 * ng + h)),
            pl.BlockSpec((1, tq, gw), lambda b, h, i: (b, i, 3 * ng + h)),
            pl.BlockSpec((1, LANES), lambda b, h, i: (0, 0)),
        ],
        out_specs=pl.BlockSpec((1, tq, gw), lambda b, h, i: (b, i, h)),
        out_shape=jax.ShapeDtypeStruct((bsz, seq, W_ATTN), BF16),
        scratch_shapes=[pltpu.VMEM((ATTN_GROUP, seq // tq, 2 * tq, tq), F32),
                        pltpu.VMEM((ATTN_GROUP, 2 * tq, LANES), F32),
                        pltpu.VMEM((ATTN_GROUP, 2 * tq, 2 * LANES), F32)],
        compiler_params=pltpu.CompilerParams(
            dimension_semantics=("parallel", "parallel", "arbitrary"),
            vmem_limit_bytes=VMEM_LIMIT),
        name="diff_attn",
    )(lam_params, proj_a, proj_a, proj_a, proj_a, subln_g)


def _split3(x):
    hi = x.astype(BF16)
    r1 = x - hi.astype(F32)
    mid = r1.astype(BF16)
    lo = (r1 - mid.astype(F32)).astype(BF16)
    return hi, mid, lo


def _rwkv_kernel(pb_ref, mu_ref, w0_ref, w2_ref, a0_ref, a2_ref, kk_ref, ka_ref, rk_ref,
                 lng_ref, lnb_ref, o_ref, prev_ref, state_ref, *, nchunk):
    c = CHUNK
    hd = RWKV_HEAD_DIM
    tt = nchunk * c

    @pl.when(pl.program_id(1) == 0)
    def _():
        prev_ref[...] = jnp.zeros_like(prev_ref)
        state_ref[...] = jnp.zeros_like(state_ref)

    z = pb_ref[0].astype(F32)
    zs = z[:, :SHIFT_WIDTH]
    row = lax.broadcasted_iota(jnp.int32, (tt, 1), 0)
    prev = jnp.where(row == 0, prev_ref[...], pltpu.roll(zs, 1, axis=0))
    prev_ref[...] = zs[tt - 1:tt, :]
    xs = zs + (prev - zs) * mu_ref[...]

    r = xs[:, 0:W_RWKV]
    kb = xs[:, W_RWKV:2 * W_RWKV]
    vb = xs[:, 2 * W_RWKV:3 * W_RWKV]
    lora_in = xs[:, 3 * W_RWKV:SHIFT_WIDTH]
    g = z[:, SHIFT_WIDTH:]

    u = w0_ref[...] + _dot(jnp.tanh(lora_in).astype(BF16), w2_ref[...])
    w_log = -(jnp.maximum(-u, 0.0) + jnp.log(1.0 + jnp.exp(-jnp.abs(u)))) - 0.5
    logw = -jnp.exp(w_log)
    a = 1.0 / (1.0 + jnp.exp(-(a0_ref[...] + _dot(lora_in.astype(BF16), a2_ref[...]))))

    lane4 = lax.broadcasted_iota(jnp.int32, (1, W_RWKV), 1)
    first4 = (lane4 % LANES) < hd

    def head_sum(x):
        outs = []
        for p in range(PAIRS):
            xp = x[:, p * LANES:(p + 1) * LANES]
            fp = first4[:, p * LANES:(p + 1) * LANES]
            s0 = jnp.sum(jnp.where(fp, xp, 0.0), axis=-1, keepdims=True)
            s1 = jnp.sum(jnp.where(fp, 0.0, xp), axis=-1, keepdims=True)
            outs.append(jnp.where(fp, s0, s1))
        return jnp.concatenate(outs, axis=1)

    kk = kb * kk_ref[...]
    kk = kk * lax.rsqrt(jnp.maximum(head_sum(kk * kk), 1e-24))
    k_mod = kb * (1.0 + (a - 1.0) * ka_ref[...])
    bonus = head_sum(r * k_mod * rk_ref[...]) * vb

    ri = lax.broadcasted_iota(jnp.int32, (tt, tt), 0)
    ci = lax.broadcasted_iota(jnp.int32, (tt, tt), 1)
    tri = ((ci <= ri) & ((ci // c) == (ri // c))).astype(BF16)
    hi, mid, lo = _split3(logw)
    log2e = math.log2(math.e)
    cs = (_dot(tri, hi) + _dot(tri, mid) + _dot(tri, lo)) * log2e
    c_ends = [cs[k * c + c - 1:k * c + c, :] for k in range(nchunk)]
    c_end = jnp.concatenate([jnp.broadcast_to(ce, (c, W_RWKV)) for ce in c_ends], axis=0)

    e_neg = jnp.exp2(-cs)
    e_end = jnp.exp2(c_end - cs)
    kka = kk * a
    at_all = -kk * jnp.exp2(cs - logw * log2e)
    bt_all = kka * e_neg
    kt_all = k_mod * e_neg
    rt_all = r * jnp.exp2(cs)
    bp_all = kka * e_end
    kp_all = k_mod * e_end
    pc_all = [jnp.exp2(ce) for ce in c_ends]

    lane = lax.broadcasted_iota(jnp.int32, (1, LANES), 1)
    first = lane < hd
    first2 = jnp.concatenate([first, first], axis=1)
    rowc = lax.broadcasted_iota(jnp.int32, (c, LANES), 0)
    colc = lax.broadcasted_iota(jnp.int32, (c, LANES), 1) % hd
    strict = colc < rowc
    incl = colc <= rowc
    r2 = lax.broadcasted_iota(jnp.int32, (LANES, LANES), 0)
    c2 = lax.broadcasted_iota(jnp.int32, (LANES, LANES), 1)
    same_head = (r2 // hd) == (c2 // hd)
    eye = r2 == c2

    def lower_blk(s):
        return ((r2 // (2 * s)) == (c2 // (2 * s))) & ((r2 // s) % 2 == 1) & ((c2 // s) % 2 == 0)
    zeros_cl = jnp.zeros((c, LANES), F32)

    def h0(x):
        return jnp.where(first, x, 0.0)

    def h1(x):
        return jnp.where(first, 0.0, x)

    units = [(k, p) for k in range(nchunk) for p in range(PAIRS)]

    def cut(x, k, p):
        return x[k * c:(k + 1) * c, p * LANES:(p + 1) * LANES]

    at = [cut(at_all, k, p) for k, p in units]
    rt = [cut(rt_all, k, p) for k, p in units]
    vm = [cut(vb, k, p) for k, p in units]

    mm = []
    for n, (k, p) in enumerate(units):
        bt, kt = cut(bt_all, k, p), cut(kt_all, k, p)
        ar = jnp.concatenate([at[n], rt[n]], axis=0).astype(BF16)
        rh = jnp.concatenate([h0(bt), h0(kt), h1(kt), h1(bt)], axis=0).astype(BF16)
        mm.append(_dot_nt(ar, rh))
    la0 = [jnp.where(strict, m[:c, :LANES], 0.0) for m in mm]
    la1 = [jnp.where(strict, m[:c, LANES:], 0.0) for m in mm]
    mr0 = [jnp.where(incl, m[c:, :LANES], 0.0) for m in mm]
    mr1 = [jnp.where(incl, m[c:, LANES:], 0.0) for m in mm]

    lakv = [_dot(jnp.where(first, la1[n], la0[n]).astype(BF16),
                 jnp.concatenate([h1(vm[n]), h0(vm[n])], axis=0).astype(BF16))
            for n in range(len(units))]

    lbd = [jnp.concatenate([h0(la0[n]), h1(la1[n])], axis=0) for n in range(len(units))]
    tinv = [(jnp.where(eye, 1.0, 0.0) + jnp.where(lower_blk(1), l, 0.0)).astype(BF16) for l in lbd]
    lbd16 = [l.astype(BF16) for l in lbd]
    sz = 2
    while sz < c:
        msk = lower_blk(sz)
        inner = [_dot(l, t).astype(BF16) for l, t in zip(lbd16, tinv)]
        tinv = [jnp.where(msk, _dot(t, m).astype(BF16), t) for t, m in zip(tinv, inner)]
        sz *= 2
    tz = []
    for n in range(len(units)):
        zc = jnp.concatenate([at[n], lakv[n]], axis=1)
        x = jnp.concatenate([jnp.where(first2, zc, 0.0), jnp.where(first2, 0.0, zc)], axis=0)
        xs2 = _dot(tinv[n], x.astype(BF16))
        tz.append(xs2[:c] + xs2[c:])
    w = [t[:, :LANES] for t in tz]
    u0 = [t[:, LANES:] for t in tz]

    gmat, hmat = [], []
    for n, (k, p) in enumerate(units):
        bp, kp = cut(bp_all, k, p), cut(kp_all, k, p)
        lt = jnp.concatenate([bp, kp], axis=0).T.astype(BF16)
        rt2 = jnp.concatenate([jnp.concatenate([w[n], u0[n]], axis=1),
                               jnp.concatenate([zeros_cl, vm[n]], axis=1)], axis=0).astype(BF16)
        gh = _dot(lt, rt2)
        pc = pc_all[k][:, p * LANES:(p + 1) * LANES]
        gmat.append(jnp.where(same_head, gh[:, :LANES], 0.0) + jnp.where(eye, pc, 0.0))
        hmat.append(jnp.where(same_head, gh[:, LANES:], 0.0))

    qh, y0 = [], []
    for n in range(len(units)):
        lq = jnp.concatenate([jnp.where(first, mr0[n], mr1[n]), jnp.where(first, mr1[n], mr0[n])],
                             axis=1).astype(BF16)
        rq = jnp.concatenate([
            jnp.concatenate([h0(w[n]), h0(u0[n])], axis=1),
            jnp.concatenate([h1(w[n]), h1(u0[n])], axis=1),
            jnp.concatenate([zeros_cl, h1(vm[n])], axis=1),
            jnp.concatenate([zeros_cl, h0(vm[n])], axis=1)], axis=0).astype(BF16)
        qy = _dot(lq, rq)
        qh.append(rt[n] + qy[:, :LANES])
        y0.append(qy[:, LANES:])

    state = [state_ref[p] for p in range(PAIRS)]
    ys = []
    for n, (k, p) in enumerate(units):
        gs = _dot(jnp.concatenate([gmat[n], qh[n]], axis=0).astype(BF16), state[p].astype(BF16))
        state[p] = gs[:LANES] + hmat[n]
        ys.append(gs[LANES:] + y0[n])
    for p in range(PAIRS):
        state_ref[p] = state[p]

    def pair_mean(t):
        s0 = jnp.sum(h0(t), axis=-1, keepdims=True)
        s1 = jnp.sum(h1(t), axis=-1, keepdims=True)
        return jnp.where(first, s0, s1) * (1.0 / hd)

    for n, (k, p) in enumerate(units):
        sl = slice(p * LANES, (p + 1) * LANES)
        y = ys[n]
        mu = pair_mean(y)
        d = y - mu
        var = pair_mean(d * d)
        yn = d * lax.rsqrt(var + GN_EPS) * lng_ref[:, sl] + lnb_ref[:, sl]
        gp = cut(g, k, p)
        o_ref[0, k * c:(k + 1) * c, sl] = (
            (yn + cut(bonus, k, p)) * (gp / (1.0 + jnp.exp(-gp)))).astype(o_ref.dtype)


def _rwkv(proj_b, mu, w0, w2p, a0, a2p, k_k, k_a, r_k, ln_g, ln_b, bsz, seq, nchunk):
    tt = nchunk * CHUNK
    vec = lambda n: pl.BlockSpec((1, n), lambda b, t: (0, 0))
    mat = pl.BlockSpec((LANES, W_RWKV), lambda b, t: (0, 0))
    return pl.pallas_call(
        functools.partial(_rwkv_kernel, nchunk=nchunk),
        grid=(bsz, seq // tt),
        in_specs=[
            pl.BlockSpec((1, tt, B_COLS), lambda b, t: (b, t, 0)),
            vec(SHIFT_WIDTH), vec(W_RWKV), mat, vec(W_RWKV), mat,
            vec(W_RWKV), vec(W_RWKV), vec(W_RWKV), vec(W_RWKV), vec(W_RWKV),
        ],
        out_specs=pl.BlockSpec((1, tt, W_RWKV), lambda b, t: (b, t, 0)),
        out_shape=jax.ShapeDtypeStruct((bsz, seq, W_RWKV), BF16),
        scratch_shapes=[
            pltpu.VMEM((1, SHIFT_WIDTH), F32),
            pltpu.VMEM((PAIRS, LANES, LANES), F32),
        ],
        compiler_params=pltpu.CompilerParams(
            dimension_semantics=("parallel", "arbitrary"), vmem_limit_bytes=VMEM_LIMIT),
        name="rwkv7",
    )(proj_b, mu, w0, w2p, a0, a2p, k_k, k_a, r_k, ln_g, ln_b)


def _outproj_kernel(x_ref, oa_ref, ob_ref, wa_ref, wb_ref, fg_ref, o_ref, *, final):
    h = x_ref[...] + _dot(oa_ref[...], wa_ref[...]) + _dot(ob_ref[...], wb_ref[...])
    if final:
        ms = jnp.mean(h * h, axis=-1, keepdims=True)
        h = h * lax.rsqrt(ms + RMS_EPS) * fg_ref[...]
    o_ref[...] = h


def _outproj(x2d, oa, ob, w_oa, w_ob, final_g, tm, final):
    m = x2d.shape[0]
    return pl.pallas_call(
        functools.partial(_outproj_kernel, final=final),
        grid=(m // tm,),
        in_specs=[
            pl.BlockSpec((tm, D_MODEL), lambda i: (i, 0)),
            pl.BlockSpec((tm, W_ATTN), lambda i: (i, 0)),
            pl.BlockSpec((tm, W_RWKV), lambda i: (i, 0)),
            pl.BlockSpec((W_ATTN, D_MODEL), lambda i: (0, 0)),
            pl.BlockSpec((W_RWKV, D_MODEL), lambda i: (0, 0)),
            pl.BlockSpec((1, D_MODEL), lambda i: (0, 0)),
        ],
        out_specs=pl.BlockSpec((tm, D_MODEL), lambda i: (i, 0)),
        out_shape=jax.ShapeDtypeStruct((m, D_MODEL), F32),
        compiler_params=pltpu.CompilerParams(
            dimension_semantics=("parallel",), vmem_limit_bytes=VMEM_LIMIT),
        name="outproj",
    )(x2d, oa, ob, w_oa, w_ob, final_g)


def _pad_rows(w, top):
    zeros = jnp.zeros_like(w)
    return jnp.concatenate([w, zeros] if top else [zeros, w], axis=0)


def kernel(x, norm_g, w_in, lambda_q1, lambda_k1, lambda_q2, lambda_k2, subln_g,
           shift_mu, w0, w2, a0, a2, k_k, k_a, r_k, ln_x_g, ln_x_b, w_out, final_g):
    bsz, seq, d = x.shape
    depth = w_in.shape[0]
    assert d == D_MODEL and w_in.shape[2] == D_IN
    m = bsz * seq
    tm = min(512, m)
    tq = min(512, seq)
    nchunk = 4
    assert m % tm == 0 and seq % tq == 0 and seq % (nchunk * CHUNK) == 0

    row = lambda v: v.reshape(1, -1).astype(F32)
    h = x.reshape(m, d)
    for l in range(depth):
        proj_a, proj_b = _inproj(h, row(norm_g[l]), w_in[l, :, :A_COLS].astype(BF16),
                                 w_in[l, :, A_COLS:].astype(BF16), tm)
        lam_params = jnp.stack([lambda_q1[l], lambda_k1[l], lambda_q2[l], lambda_k2[l]]).astype(F32)
        o_a = _attention(proj_a.reshape(bsz, seq, A_COLS), lam_params, row(subln_g[l]),
                         bsz, seq, tq, _lambda_init(l))
        o_b = _rwkv(proj_b.reshape(bsz, seq, B_COLS), row(shift_mu[l]), row(w0[l]),
                    _pad_rows(w2[l], True).astype(BF16), row(a0[l]),
                    _pad_rows(a2[l], False).astype(BF16), row(k_k[l]), row(k_a[l]),
                    row(r_k[l]), row(ln_x_g[l]), row(ln_x_b[l]), bsz, seq, nchunk)
        h = _outproj(h, o_a.reshape(m, W_ATTN), o_b.reshape(m, W_RWKV),
                     w_out[l, :W_ATTN].astype(BF16), w_out[l, W_ATTN:].astype(BF16),
                     row(final_g), tm, final=(l == depth - 1))
    return h.reshape(bsz, seq, d)
```

```python
import functools
import math

import jax
import jax.numpy as jnp
from jax import lax
from jax.experimental import pallas as pl
from jax.experimental.pallas import tpu as pltpu

F32 = jnp.float32
BF16 = jnp.bfloat16

D_MODEL = 1024
CHUNK = 64
W_ATTN = D_MODEL // 2
W_RWKV = D_MODEL - W_ATTN
ATTN_HEAD_DIM = 64
ATTN_HEADS = W_ATTN // (2 * ATTN_HEAD_DIM)
RWKV_HEAD_DIM = 64
RWKV_HEADS = W_RWKV // RWKV_HEAD_DIM
LORA = 64
RMS_EPS = 1e-6
SUBLN_EPS = 1e-5
GN_EPS = 1e-5 * RWKV_HEAD_DIM
A_COLS = 4 * W_ATTN
SHIFT_WIDTH = 3 * W_RWKV + 2 * LORA
B_COLS = SHIFT_WIDTH + W_RWKV
D_IN = A_COLS + B_COLS

LANES = 128
PAIRS = W_RWKV // LANES
VMEM_LIMIT = 56 * 1024 * 1024


def _lambda_init(layer_idx):
    return 0.8 - 0.6 * math.exp(-0.3 * layer_idx)


def _dot(a, b):
    return jnp.dot(a, b, preferred_element_type=F32)


def _dot_nt(a, b):
    return lax.dot_general(a, b, (((1,), (1,)), ((), ())), preferred_element_type=F32)


def _inproj_kernel(x_ref, g_ref, wa_ref, wb_ref, oa_ref, ob_ref):
    x = x_ref[...]
    ms = jnp.mean(x * x, axis=-1, keepdims=True)
    xn = (x * lax.rsqrt(ms + RMS_EPS) * g_ref[...]).astype(BF16)
    q_scale = ATTN_HEAD_DIM ** -0.5 * math.log2(math.e)
    oa_ref[:, :W_ATTN] = (_dot(xn, wa_ref[:, :W_ATTN]) * q_scale).astype(oa_ref.dtype)
    oa_ref[:, W_ATTN:] = _dot(xn, wa_ref[:, W_ATTN:]).astype(oa_ref.dtype)
    ob_ref[...] = _dot(xn, wb_ref[...]).astype(ob_ref.dtype)


def _inproj(x2d, norm_g, w_a, w_b, tm):
    m = x2d.shape[0]
    return pl.pallas_call(
        _inproj_kernel,
        grid=(m // tm,),
        in_specs=[
            pl.BlockSpec((tm, D_MODEL), lambda i: (i, 0)),
            pl.BlockSpec((1, D_MODEL), lambda i: (0, 0)),
            pl.BlockSpec((D_MODEL, A_COLS), lambda i: (0, 0)),
            pl.BlockSpec((D_MODEL, B_COLS), lambda i: (0, 0)),
        ],
        out_specs=[
            pl.BlockSpec((tm, A_COLS), lambda i: (i, 0)),
            pl.BlockSpec((tm, B_COLS), lambda i: (i, 0)),
        ],
        out_shape=[
            jax.ShapeDtypeStruct((m, A_COLS), BF16),
            jax.ShapeDtypeStruct((m, B_COLS), BF16),
        ],
        compiler_params=pltpu.CompilerParams(
            dimension_semantics=("parallel",), vmem_limit_bytes=VMEM_LIMIT),
        name="inproj",
    )(x2d, norm_g, w_a, w_b)


ATTN_GROUP = 2


def _attn_kernel(lam_ref, q_ref, k_ref, v_ref, g_ref, sg_ref, o_ref, s_scr, m_scr, acc_scr,
                 *, tq, lam_init):
    i = pl.program_id(2)
    lane = lax.broadcasted_iota(jnp.int32, (1, LANES), 1)
    first = lane < ATTN_HEAD_DIM
    heads = range(ATTN_GROUP)

    lp = lam_ref[...]
    l1 = jnp.sum(lp[0:1] * lp[1:2], axis=-1, keepdims=True)
    l2 = jnp.sum(lp[2:3] * lp[3:4], axis=-1, keepdims=True)
    lam = jnp.exp(l1) - jnp.exp(l2) + lam_init

    def head(ref, hh, rows=slice(None)):
        return ref[0, rows, hh * LANES:(hh + 1) * LANES]

    qs = []
    for hh in heads:
        q = head(q_ref, hh)
        zero = jnp.zeros_like(q)
        qs.append(jnp.concatenate([jnp.where(first, q, zero), jnp.where(first, zero, q)], axis=0))

    nt = tq // LANES

    def tiles(s):
        return [s[:, t * LANES:(t + 1) * LANES] for t in range(nt)]

    def scores(hh, j):
        start = pl.multiple_of(j * tq, tq)
        s = _dot_nt(qs[hh], head(k_ref, hh, pl.ds(start, tq)))
        s_scr[hh, j] = s
        return functools.reduce(jnp.maximum, tiles(s))

    diag = pl.multiple_of(i * tq, tq)
    sub_row = lax.broadcasted_iota(jnp.int32, (2 * LANES, 1), 0) % LANES
    hidden = (sub_row < CHUNK) & (lane >= CHUNK)

    def sub_rows(u):
        return slice(u * LANES, (u + 1) * LANES), slice(tq + u * LANES, tq + (u + 1) * LANES)

    def diag_scores(hh):
        for u in range(nt):
            r1, r2 = sub_rows(u)
            nk = (u + 1) * LANES
            s = _dot_nt(jnp.concatenate([qs[hh][r1], qs[hh][r2]], axis=0),
                        head(k_ref, hh, pl.ds(diag, nk)))
            parts = [s[:, t * LANES:(t + 1) * LANES] for t in range(u)]
            parts.append(jnp.where(hidden, -jnp.inf, s[:, u * LANES:nk]))
            for t, part in enumerate(parts):
                s_scr[hh, i, r1, t * LANES:(t + 1) * LANES] = part[:LANES]
                s_scr[hh, i, r2, t * LANES:(t + 1) * LANES] = part[LANES:]
            m = functools.reduce(jnp.maximum, parts)
            m_scr[hh, r1] = m[:LANES]
            m_scr[hh, r2] = m[LANES:]

    for hh in heads:
        diag_scores(hh)

    n_pairs = lax.shift_right_logical(i, 1)
    leftover = jnp.bitwise_and(i, 1) == 1

    def pass1_block(j):
        for hh in heads:
            m_scr[hh] = jnp.maximum(m_scr[hh], scores(hh, j))

    def pass1(jj, carry):
        pass1_block(2 * jj)
        pass1_block(2 * jj + 1)
        return carry

    lax.fori_loop(0, n_pairs, pass1, 0)
    pl.when(leftover)(lambda: pass1_block(i - 1))
    mb = [jnp.broadcast_to(jnp.max(m_scr[hh], axis=-1, keepdims=True), (2 * tq, LANES))
          for hh in heads]

    def with_ones(v):
        return jnp.concatenate([v, jnp.ones_like(v)], axis=1)

    def probs_v(hh, j):
        start = pl.multiple_of(j * tq, tq)
        ps = [jnp.exp2(t - mb[hh]).astype(BF16) for t in tiles(s_scr[hh, j])]
        return _dot(jnp.concatenate(ps, axis=1), with_ones(head(v_ref, hh, pl.ds(start, tq))))

    def diag_probs_v(hh):
        for u in range(nt):
            r1, r2 = sub_rows(u)
            nk = (u + 1) * LANES
            mbu = jnp.concatenate([mb[hh][r1], mb[hh][r2]], axis=0)
            ps = [jnp.exp2(jnp.concatenate([s_scr[hh, i, r1, t * LANES:(t + 1) * LANES],
                                            s_scr[hh, i, r2, t * LANES:(t + 1) * LANES]], axis=0)
                           - mbu).astype(BF16) for t in range(u + 1)]
            pv = _dot(jnp.concatenate(ps, axis=1),
                      with_ones(head(v_ref, hh, pl.ds(diag, nk))))
            acc_scr[hh, r1], acc_scr[hh, r2] = pv[:LANES], pv[LANES:]

    for hh in heads:
        diag_probs_v(hh)

    def pass2_block(j):
        for hh in heads:
            acc_scr[hh] += probs_v(hh, j)

    def pass2(jj, carry):
        pass2_block(2 * jj)
        pass2_block(2 * jj + 1)
        return carry

    lax.fori_loop(0, n_pairs, pass2, 0)
    pl.when(leftover)(lambda: pass2_block(i - 1))

    for hh in heads:
        acc = acc_scr[hh]
        o = acc[:, :LANES] / acc[:, LANES:]
        o = o[:tq] - lam * o[tq:]
        ms = jnp.mean(o * o, axis=-1, keepdims=True)
        o = o * lax.rsqrt(ms + SUBLN_EPS) * sg_ref[...] * (1.0 - lam_init)
        g = head(g_ref, hh).astype(F32)
        o_ref[0, :, hh * LANES:(hh + 1) * LANES] = (
            o * (g / (1.0 + jnp.exp(-g)))).astype(o_ref.dtype)


def _attention(proj_a, lam_params, subln_g, bsz, seq, tq, lam_init):
    ng = ATTN_HEADS // ATTN_GROUP
    gw = ATTN_GROUP * LANES
    kernel = functools.partial(_attn_kernel, tq=tq, lam_init=lam_init)
    return pl.pallas_call(
        kernel,
        grid=(bsz, ng, seq // tq),
        in_specs=[
            pl.BlockSpec((4, ATTN_HEAD_DIM), lambda b, h, i: (0, 0)),
            pl.BlockSpec((1, tq, gw), lambda b, h, i: (b, i, h)),
            pl.BlockSpec((1, seq, gw), lambda b, h, i: (b, 0, ng + h)),
            pl.BlockSpec((1, seq, gw), lambda b, h, i: (b, 0, 2 * ng + h)),
            pl.BlockSpec((1, tq, gw), lambda b, h, i: (b, i, 3 * ng + h)),
            pl.BlockSpec((1, LANES), lambda b, h, i: (0, 0)),
        ],
        out_specs=pl.BlockSpec((1, tq, gw), lambda b, h, i: (b, i, h)),
        out_shape=jax.ShapeDtypeStruct((bsz, seq, W_ATTN), BF16),
        scratch_shapes=[pltpu.VMEM((ATTN_GROUP, seq // tq, 2 * tq, tq), F32),
                        pltpu.VMEM((ATTN_GROUP, 2 * tq, LANES), F32),
                        pltpu.VMEM((ATTN_GROUP, 2 * tq, 2 * LANES), F32)],
        compiler_params=pltpu.CompilerParams(
            dimension_semantics=("parallel", "parallel", "arbitrary"),
            vmem_limit_bytes=VMEM_LIMIT),
        name="diff_attn",
    )(lam_params, proj_a, proj_a, proj_a, proj_a, subln_g)


def _split3(x):
    hi = x.astype(BF16)
    r1 = x - hi.astype(F32)
    mid = r1.astype(BF16)
    lo = (r1 - mid.astype(F32)).astype(BF16)
    return hi, mid, lo


def _rwkv_kernel(pb_ref, mu_ref, w0_ref, w2_ref, a0_ref, a2_ref, kk_ref, ka_ref, rk_ref,
                 lng_ref, lnb_ref, o_ref, prev_ref, state_ref, *, nchunk):
    c = CHUNK
    hd = RWKV_HEAD_DIM
    tt = nchunk * c

    @pl.when(pl.program_id(1) == 0)
    def _():
        prev_ref[...] = jnp.zeros_like(prev_ref)
        state_ref[...] = jnp.zeros_like(state_ref)

    z = pb_ref[0].astype(F32)
    zs = z[:, :SHIFT_WIDTH]
    row = lax.broadcasted_iota(jnp.int32, (tt, 1), 0)
    prev = jnp.where(row == 0, prev_ref[...], pltpu.roll(zs, 1, axis=0))
    prev_ref[...] = zs[tt - 1:tt, :]
    xs = zs + (prev - zs) * mu_ref[...]

    r = xs[:, 0:W_RWKV]
    kb = xs[:, W_RWKV:2 * W_RWKV]
    vb = xs[:, 2 * W_RWKV:3 * W_RWKV]
    lora_in = xs[:, 3 * W_RWKV:SHIFT_WIDTH]
    g = z[:, SHIFT_WIDTH:]

    u = w0_ref[...] + _dot(jnp.tanh(lora_in).astype(BF16), w2_ref[...])
    w_log = -(jnp.maximum(-u, 0.0) + jnp.log(1.0 + jnp.exp(-jnp.abs(u)))) - 0.5
    logw = -jnp.exp(w_log)
    a = 1.0 / (1.0 + jnp.exp(-(a0_ref[...] + _dot(lora_in.astype(BF16), a2_ref[...]))))

    lane4 = lax.broadcasted_iota(jnp.int32, (1, W_RWKV), 1)
    first4 = (lane4 % LANES) < hd

    def head_sum(x):
        outs = []
        for p in range(PAIRS):
            xp = x[:, p * LANES:(p + 1) * LANES]
            fp = first4[:, p * LANES:(p + 1) * LANES]
            s0 = jnp.sum(jnp.where(fp, xp, 0.0), axis=-1, keepdims=True)
            s1 = jnp.sum(jnp.where(fp, 0.0, xp), axis=-1, keepdims=True)
            outs.append(jnp.where(fp, s0, s1))
        return jnp.concatenate(outs, axis=1)

    kk = kb * kk_ref[...]
    kk = kk * lax.rsqrt(jnp.maximum(head_sum(kk * kk), 1e-24))
    k_mod = kb * (1.0 + (a - 1.0) * ka_ref[...])
    bonus = head_sum(r * k_mod * rk_ref[...]) * vb

    ri = lax.broadcasted_iota(jnp.int32, (tt, tt), 0)
    ci = lax.broadcasted_iota(jnp.int32, (tt, tt), 1)
    tri = ((ci <= ri) & ((ci // c) == (ri // c))).astype(BF16)
    hi, mid, lo = _split3(logw)
    log2e = math.log2(math.e)
    cs = (_dot(tri, hi) + _dot(tri, mid) + _dot(tri, lo)) * log2e
    c_ends = [cs[k * c + c - 1:k * c + c, :] for k in range(nchunk)]
    c_end = jnp.concatenate([jnp.broadcast_to(ce, (c, W_RWKV)) for ce in c_ends], axis=0)

    e_neg = jnp.exp2(-cs)
    e_end = jnp.exp2(c_end - cs)
    kka = kk * a
    at_all = -kk * jnp.exp2(cs - logw * log2e)
    bt_all = kka * e_neg
    kt_all = k_mod * e_neg
    rt_all = r * jnp.exp2(cs)
    bp_all = kka * e_end
    kp_all = k_mod * e_end
    pc_all = [jnp.exp2(ce) for ce in c_ends]

    lane = lax.broadcasted_iota(jnp.int32, (1, LANES), 1)
    first = lane < hd
    first2 = jnp.concatenate([first, first], axis=1)
    rowc = lax.broadcasted_iota(jnp.int32, (c, LANES), 0)
    colc = lax.broadcasted_iota(jnp.int32, (c, LANES), 1) % hd
    strict = colc < rowc
    incl = colc <= rowc
    r2 = lax.broadcasted_iota(jnp.int32, (LANES, LANES), 0)
    c2 = lax.broadcasted_iota(jnp.int32, (LANES, LANES), 1)
    same_head = (r2 // hd) == (c2 // hd)
    eye = r2 == c2

    def lower_blk(s):
        return ((r2 // (2 * s)) == (c2 // (2 * s))) & ((r2 // s) % 2 == 1) & ((c2 // s) % 2 == 0)
    zeros_cl = jnp.zeros((c, LANES), F32)

    def h0(x):
        return jnp.where(first, x, 0.0)

    def h1(x):
        return jnp.where(first, 0.0, x)

    units = [(k, p) for k in range(nchunk) for p in range(PAIRS)]

    def cut(x, k, p):
        return x[k * c:(k + 1) * c, p * LANES:(p + 1) * LANES]

    at = [cut(at_all, k, p) for k, p in units]
    rt = [cut(rt_all, k, p) for k, p in units]
    vm = [cut(vb, k, p) for k, p in units]

    mm = []
    for n, (k, p) in enumerate(units):
        bt, kt = cut(bt_all, k, p), cut(kt_all, k, p)
        ar = jnp.concatenate([at[n], rt[n]], axis=0).astype(BF16)
        rh = jnp.concatenate([h0(bt), h0(kt), h1(kt), h1(bt)], axis=0).astype(BF16)
        mm.append(_dot_nt(ar, rh))
    la0 = [jnp.where(strict, m[:c, :LANES], 0.0) for m in mm]
    la1 = [jnp.where(strict, m[:c, LANES:], 0.0) for m in mm]
    mr0 = [jnp.where(incl, m[c:, :LANES], 0.0) for m in mm]
    mr1 = [jnp.where(incl, m[c:, LANES:], 0.0) for m in mm]

    lakv = [_dot(jnp.where(first, la1[n], la0[n]).astype(BF16),
                 jnp.concatenate([h1(vm[n]), h0(vm[n])], axis=0).astype(BF16))
            for n in range(len(units))]

    lbd = [jnp.concatenate([h0(la0[n]), h1(la1[n])], axis=0) for n in range(len(units))]
    tinv = [(jnp.where(eye, 1.0, 0.0) + jnp.where(lower_blk(1), l, 0.0)).astype(BF16) for l in lbd]
    lbd16 = [l.astype(BF16) for l in lbd]
    sz = 2
    while sz < c:
        msk = lower_blk(sz)
        inner = [_dot(l, t).astype(BF16) for l, t in zip(lbd16, tinv)]
        tinv = [jnp.where(msk, _dot(t, m).astype(BF16), t) for t, m in zip(tinv, inner)]
        sz *= 2
    tz = []
    for n in range(len(units)):
        zc = jnp.concatenate([at[n], lakv[n]], axis=1)
        x = jnp.concatenate([jnp.where(first2, zc, 0.0), jnp.where(first2, 0.0, zc)], axis=0)
        xs2 = _dot(tinv[n], x.astype(BF16))
        tz.append(xs2[:c] + xs2[c:])
    w = [t[:, :LANES] for t in tz]
    u0 = [t[:, LANES:] for t in tz]

    gmat, hmat = [], []
    for n, (k, p) in enumerate(units):
        bp, kp = cut(bp_all, k, p), cut(kp_all, k, p)
        lt = jnp.concatenate([bp, kp], axis=0).T.astype(BF16)
        rt2 = jnp.concatenate([jnp.concatenate([w[n], u0[n]], axis=1),
                               jnp.concatenate([zeros_cl, vm[n]], axis=1)], axis=0).astype(BF16)
        gh = _dot(lt, rt2)
        pc = pc_all[k][:, p * LANES:(p + 1) * LANES]
        gmat.append(jnp.where(same_head, gh[:, :LANES], 0.0) + jnp.where(eye, pc, 0.0))
        hmat.append(jnp.where(same_head, gh[:, LANES:], 0.0))

    qh, y0 = [], []
    for n in range(len(units)):
        lq = jnp.concatenate([jnp.where(first, mr0[n], mr1[n]), jnp.where(first, mr1[n], mr0[n])],
                             axis=1).astype(BF16)
        rq = jnp.concatenate([
            jnp.concatenate([h0(w[n]), h0(u0[n])], axis=1),
            jnp.concatenate([h1(w[n]), h1(u0[n])], axis=1),
            jnp.concatenate([zeros_cl, h1(vm[n])], axis=1),
            jnp.concatenate([zeros_cl, h0(vm[n])], axis=1)], axis=0).astype(BF16)
        qy = _dot(lq, rq)
        qh.append(rt[n] + qy[:, :LANES])
        y0.append(qy[:, LANES:])

    state = [state_ref[p] for p in range(PAIRS)]
    ys = []
    for n, (k, p) in enumerate(units):
        gs = _dot(jnp.concatenate([gmat[n], qh[n]], axis=0).astype(BF16), state[p].astype(BF16))
        state[p] = gs[:LANES] + hmat[n]
        ys.append(gs[LANES:] + y0[n])
    for p in range(PAIRS):
        state_ref[p] = state[p]

    def pair_mean(t):
        s0 = jnp.sum(h0(t), axis=-1, keepdims=True)
        s1 = jnp.sum(h1(t), axis=-1, keepdims=True)
        return jnp.where(first, s0, s1) * (1.0 / hd)

    for n, (k, p) in enumerate(units):
        sl = slice(p * LANES, (p + 1) * LANES)
        y = ys[n]
        mu = pair_mean(y)
        d = y - mu
        var = pair_mean(d * d)
        yn = d * lax.rsqrt(var + GN_EPS) * lng_ref[:, sl] + lnb_ref[:, sl]
        gp = cut(g, k, p)
        o_ref[0, k * c:(k + 1) * c, sl] = (
            (yn + cut(bonus, k, p)) * (gp / (1.0 + jnp.exp(-gp)))).astype(o_ref.dtype)


def _rwkv(proj_b, mu, w0, w2p, a0, a2p, k_k, k_a, r_k, ln_g, ln_b, bsz, seq, nchunk):
    tt = nchunk * CHUNK
    vec = lambda n: pl.BlockSpec((1, n), lambda b, t: (0, 0))
    mat = pl.BlockSpec((LANES, W_RWKV), lambda b, t: (0, 0))
    return pl.pallas_call(
        functools.partial(_rwkv_kernel, nchunk=nchunk),
        grid=(bsz, seq // tt),
        in_specs=[
            pl.BlockSpec((1, tt, B_COLS), lambda b, t: (b, t, 0)),
            vec(SHIFT_WIDTH), vec(W_RWKV), mat, vec(W_RWKV), mat,
            vec(W_RWKV), vec(W_RWKV), vec(W_RWKV), vec(W_RWKV), vec(W_RWKV),
        ],
        out_specs=pl.BlockSpec((1, tt, W_RWKV), lambda b, t: (b, t, 0)),
        out_shape=jax.ShapeDtypeStruct((bsz, seq, W_RWKV), BF16),
        scratch_shapes=[
            pltpu.VMEM((1, SHIFT_WIDTH), F32),
            pltpu.VMEM((PAIRS, LANES, LANES), F32),
        ],
        compiler_params=pltpu.CompilerParams(
            dimension_semantics=("parallel", "arbitrary"), vmem_limit_bytes=VMEM_LIMIT),
        name="rwkv7",
    )(proj_b, mu, w0, w2p, a0, a2p, k_k, k_a, r_k, ln_g, ln_b)


def _outproj_kernel(x_ref, oa_ref, ob_ref, wa_ref, wb_ref, fg_ref, o_ref, *, final):
    h = x_ref[...] + _dot(oa_ref[...], wa_ref[...]) + _dot(ob_ref[...], wb_ref[...])
    if final:
        ms = jnp.mean(h * h, axis=-1, keepdims=True)
        h = h * lax.rsqrt(ms + RMS_EPS) * fg_ref[...]
    o_ref[...] = h


def _outproj(x2d, oa, ob, w_oa, w_ob, final_g, tm, final):
    m = x2d.shape[0]
    return pl.pallas_call(
        functools.partial(_outproj_kernel, final=final),
        grid=(m // tm,),
        in_specs=[
            pl.BlockSpec((tm, D_MODEL), lambda i: (i, 0)),
            pl.BlockSpec((tm, W_ATTN), lambda i: (i, 0)),
            pl.BlockSpec((tm, W_RWKV), lambda i: (i, 0)),
            pl.BlockSpec((W_ATTN, D_MODEL), lambda i: (0, 0)),
            pl.BlockSpec((W_RWKV, D_MODEL), lambda i: (0, 0)),
            pl.BlockSpec((1, D_MODEL), lambda i: (0, 0)),
        ],
        out_specs=pl.BlockSpec((tm, D_MODEL), lambda i: (i, 0)),
        out_shape=jax.ShapeDtypeStruct((m, D_MODEL), F32),
        compiler_params=pltpu.CompilerParams(
            dimension_semantics=("parallel",), vmem_limit_bytes=VMEM_LIMIT),
        name="outproj",
    )(x2d, oa, ob, w_oa, w_ob, final_g)


def _pad_rows(w, top):
    zeros = jnp.zeros_like(w)
    return jnp.concatenate([w, zeros] if top else [zeros, w], axis=0)


def kernel(x, norm_g, w_in, lambda_q1, lambda_k1, lambda_q2, lambda_k2, subln_g,
           shift_mu, w0, w2, a0, a2, k_k, k_a, r_k, ln_x_g, ln_x_b, w_out, final_g):
    bsz, seq, d = x.shape
    depth = w_in.shape[0]
    assert d == D_MODEL and w_in.shape[2] == D_IN
    m = bsz * seq
    tm_in = min(1024, m)
    tm_out = min(2048, m)
    tq = min(512, seq)
    nchunk = 4
    assert m % tm_in == 0 and m % tm_out == 0
    assert seq % tq == 0 and seq % (nchunk * CHUNK) == 0

    row = lambda v: v.reshape(1, -1).astype(F32)
    h = x.reshape(m, d)
    for l in range(depth):
        proj_a, proj_b = _inproj(h, row(norm_g[l]), w_in[l, :, :A_COLS].astype(BF16),
                                 w_in[l, :, A_COLS:].astype(BF16), tm_in)
        lam_params = jnp.stack([lambda_q1[l], lambda_k1[l], lambda_q2[l], lambda_k2[l]]).astype(F32)
        o_a = _attention(proj_a.reshape(bsz, seq, A_COLS), lam_params, row(subln_g[l]),
                         bsz, seq, tq, _lambda_init(l))
        o_b = _rwkv(proj_b.reshape(bsz, seq, B_COLS), row(shift_mu[l]), row(w0[l]),
                    _pad_rows(w2[l], True).astype(BF16), row(a0[l]),
                    _pad_rows(a2[l], False).astype(BF16), row(k_k[l]), row(k_a[l]),
                    row(r_k[l]), row(ln_x_g[l]), row(ln_x_b[l]), bsz, seq, nchunk)
        h = _outproj(h, o_a.reshape(m, W_ATTN), o_b.reshape(m, W_RWKV),
                     w_out[l, :W_ATTN].astype(BF16), w_out[l, W_ATTN:].astype(BF16),
                     row(final_g), tm_out, final=(l == depth - 1))
    return h.reshape(bsz, seq, d)
```

```python
import functools
import math

import jax
import jax.numpy as jnp
from jax import lax
from jax.experimental import pallas as pl
from jax.experimental.pallas import tpu as pltpu

F32 = jnp.float32
BF16 = jnp.bfloat16

D_MODEL = 1024
CHUNK = 64
W_ATTN = D_MODEL // 2
W_RWKV = D_MODEL - W_ATTN
ATTN_HEAD_DIM = 64
ATTN_HEADS = W_ATTN // (2 * ATTN_HEAD_DIM)
RWKV_HEAD_DIM = 64
RWKV_HEADS = W_RWKV // RWKV_HEAD_DIM
LORA = 64
RMS_EPS = 1e-6
SUBLN_EPS = 1e-5
GN_EPS = 1e-5 * RWKV_HEAD_DIM
A_COLS = 4 * W_ATTN
SHIFT_WIDTH = 3 * W_RWKV + 2 * LORA
B_COLS = SHIFT_WIDTH + W_RWKV
D_IN = A_COLS + B_COLS

LANES = 128
PAIRS = W_RWKV // LANES
VMEM_LIMIT = 56 * 1024 * 1024


def _lambda_init(layer_idx):
    return 0.8 - 0.6 * math.exp(-0.3 * layer_idx)


def _dot(a, b):
    return jnp.dot(a, b, preferred_element_type=F32)


def _dot_nt(a, b):
    return lax.dot_general(a, b, (((1,), (1,)), ((), ())), preferred_element_type=F32)


PRE_NAMES = ("r", "kk", "kka", "km", "vm", "bonus", "gate")
PREP_ROWS = 256


def _split3(x):
    hi = x.astype(BF16)
    r1 = x - hi.astype(F32)
    mid = r1.astype(BF16)
    lo = (r1 - mid.astype(F32)).astype(BF16)
    return hi, mid, lo


def _rwkv_prepare(z, prev_row, mu_ref, w0_ref, w2_ref, a0_ref, a2_ref, kk_ref, ka_ref, rk_ref):
    hd = RWKV_HEAD_DIM
    rows = z.shape[0]
    zs = z[:, :SHIFT_WIDTH]
    row = lax.broadcasted_iota(jnp.int32, (rows, 1), 0)
    prev = jnp.where(row == 0, prev_row, pltpu.roll(zs, 1, axis=0))
    xs = zs + (prev - zs) * mu_ref[...]

    r = xs[:, 0:W_RWKV]
    kb = xs[:, W_RWKV:2 * W_RWKV]
    vb = xs[:, 2 * W_RWKV:3 * W_RWKV]
    lora_in = xs[:, 3 * W_RWKV:SHIFT_WIDTH]
    g = z[:, SHIFT_WIDTH:]

    u = w0_ref[...] + _dot(jnp.tanh(lora_in).astype(BF16), w2_ref[...])
    w_log = -(jnp.maximum(-u, 0.0) + jnp.log(1.0 + jnp.exp(-jnp.abs(u)))) - 0.5
    logw = -jnp.exp(w_log)
    a = 1.0 / (1.0 + jnp.exp(-(a0_ref[...] + _dot(lora_in.astype(BF16), a2_ref[...]))))

    lane4 = lax.broadcasted_iota(jnp.int32, (1, W_RWKV), 1)
    first4 = (lane4 % LANES) < hd

    def head_sum(x):
        outs = []
        for p in range(PAIRS):
            xp = x[:, p * LANES:(p + 1) * LANES]
            fp = first4[:, p * LANES:(p + 1) * LANES]
            s0 = jnp.sum(jnp.where(fp, xp, 0.0), axis=-1, keepdims=True)
            s1 = jnp.sum(jnp.where(fp, 0.0, xp), axis=-1, keepdims=True)
            outs.append(jnp.where(fp, s0, s1))
        return jnp.concatenate(outs, axis=1)

    kk = kb * kk_ref[...]
    kk = kk * lax.rsqrt(jnp.maximum(head_sum(kk * kk), 1e-24))
    k_mod = kb * (1.0 + (a - 1.0) * ka_ref[...])
    bonus = head_sum(r * k_mod * rk_ref[...]) * vb
    pre = dict(r=r, kk=kk, kka=kk * a, km=k_mod, vm=vb, bonus=bonus,
               gate=g / (1.0 + jnp.exp(-g)))
    return pre, logw


def _decay_operands(pre, logw, nchunk):
    c = CHUNK
    rows = nchunk * c
    ri = lax.broadcasted_iota(jnp.int32, (rows, rows), 0)
    ci = lax.broadcasted_iota(jnp.int32, (rows, rows), 1)
    tri = ((ci <= ri) & ((ci // c) == (ri // c))).astype(BF16)
    hi, mid, lo = _split3(logw)
    log2e = math.log2(math.e)
    cs = (_dot(tri, hi) + _dot(tri, mid) + _dot(tri, lo)) * log2e
    c_ends = [cs[k * c + c - 1:k * c + c, :] for k in range(nchunk)]
    c_end = jnp.concatenate([jnp.broadcast_to(ce, (c, W_RWKV)) for ce in c_ends], axis=0)
    e_neg = jnp.exp2(-cs)
    e_end = jnp.exp2(c_end - cs)
    at = -pre["kk"] * jnp.exp2(cs - logw * log2e)
    rt = pre["r"] * jnp.exp2(cs)
    return (at, rt, pre["kka"] * e_neg, pre["km"] * e_neg, pre["kka"] * e_end, pre["km"] * e_end,
            [jnp.exp2(ce) for ce in c_ends])


def _inproj_kernel(x_ref, g_ref, wa_ref, wb_ref, mu_ref, w0_ref, w2_ref, a0_ref, a2_ref, kk_ref,
                   ka_ref, rk_ref, oa_ref, pre_ref, lw_ref, prev_scr, *, steps_per_seq):
    tm = x_ref.shape[0]
    q_scale = ATTN_HEAD_DIM ** -0.5 * math.log2(math.e)

    @pl.when(pl.program_id(0) % steps_per_seq == 0)
    def _():
        prev_scr[...] = jnp.zeros_like(prev_scr)

    def normed(n):
        x = x_ref[n * PREP_ROWS:(n + 1) * PREP_ROWS, :]
        ms = jnp.mean(x * x, axis=-1, keepdims=True)
        return (x * lax.rsqrt(ms + RMS_EPS) * g_ref[...]).astype(BF16)

    def project_attn(n, xn):
        rows = slice(n * PREP_ROWS, (n + 1) * PREP_ROWS)
        oa_ref[rows, :W_ATTN] = (_dot(xn, wa_ref[:, :W_ATTN]) * q_scale).astype(oa_ref.dtype)
        oa_ref[rows, W_ATTN:] = _dot(xn, wa_ref[:, W_ATTN:]).astype(oa_ref.dtype)

    def prepare(n, z, prev_row):
        rows = slice(n * PREP_ROWS, (n + 1) * PREP_ROWS)
        pre, logw = _rwkv_prepare(z, prev_row, mu_ref, w0_ref, w2_ref, a0_ref, a2_ref, kk_ref,
                                  ka_ref, rk_ref)
        for idx, name in enumerate(PRE_NAMES):
            pre_ref[idx, rows, :] = pre[name].astype(pre_ref.dtype)
        lw_ref[rows, :] = logw

    nsub = tm // PREP_ROWS
    prev_row = prev_scr[...]
    xn = normed(0)
    z = _dot(xn, wb_ref[...])
    for n in range(nsub):
        if n + 1 < nsub:
            xn_next = normed(n + 1)
            z_next = _dot(xn_next, wb_ref[...])
        prepare(n, z, prev_row)
        project_attn(n, xn)
        prev_row = z[PREP_ROWS - 1:PREP_ROWS, :SHIFT_WIDTH]
        if n + 1 < nsub:
            xn, z = xn_next, z_next
    prev_scr[...] = prev_row


def _inproj(x2d, norm_g, w_a, w_b, mu, w0, w2p, a0, a2p, k_k, k_a, r_k, tm, seq):
    m = x2d.shape[0]
    vec = lambda n: pl.BlockSpec((1, n), lambda i: (0, 0))
    mat = pl.BlockSpec((LANES, W_RWKV), lambda i: (0, 0))
    return pl.pallas_call(
        functools.partial(_inproj_kernel, steps_per_seq=seq // tm),
        grid=(m // tm,),
        in_specs=[
            pl.BlockSpec((tm, D_MODEL), lambda i: (i, 0)),
            vec(D_MODEL),
            pl.BlockSpec((D_MODEL, A_COLS), lambda i: (0, 0)),
            pl.BlockSpec((D_MODEL, B_COLS), lambda i: (0, 0)),
            vec(SHIFT_WIDTH), vec(W_RWKV), mat, vec(W_RWKV), mat,
            vec(W_RWKV), vec(W_RWKV), vec(W_RWKV),
        ],
        out_specs=[
            pl.BlockSpec((tm, A_COLS), lambda i: (i, 0)),
            pl.BlockSpec((len(PRE_NAMES), tm, W_RWKV), lambda i: (0, i, 0)),
            pl.BlockSpec((tm, W_RWKV), lambda i: (i, 0)),
        ],
        out_shape=[
            jax.ShapeDtypeStruct((m, A_COLS), BF16),
            jax.ShapeDtypeStruct((len(PRE_NAMES), m, W_RWKV), BF16),
            jax.ShapeDtypeStruct((m, W_RWKV), F32),
        ],
        scratch_shapes=[pltpu.VMEM((1, SHIFT_WIDTH), F32)],
        compiler_params=pltpu.CompilerParams(
            dimension_semantics=("arbitrary",), vmem_limit_bytes=VMEM_LIMIT),
        name="inproj",
    )(x2d, norm_g, w_a, w_b, mu, w0, w2p, a0, a2p, k_k, k_a, r_k)


ATTN_GROUP = 2


def _attn_kernel(lam_ref, q_ref, k_ref, v_ref, g_ref, sg_ref, o_ref, s_scr, m_scr, acc_scr,
                 *, tq, lam_init):
    i = pl.program_id(2)
    lane = lax.broadcasted_iota(jnp.int32, (1, LANES), 1)
    first = lane < ATTN_HEAD_DIM
    heads = range(ATTN_GROUP)

    lp = lam_ref[...]
    l1 = jnp.sum(lp[0:1] * lp[1:2], axis=-1, keepdims=True)
    l2 = jnp.sum(lp[2:3] * lp[3:4], axis=-1, keepdims=True)
    lam = jnp.exp(l1) - jnp.exp(l2) + lam_init

    def head(ref, hh, rows=slice(None)):
        return ref[0, rows, hh * LANES:(hh + 1) * LANES]

    qs = []
    for hh in heads:
        q = head(q_ref, hh)
        zero = jnp.zeros_like(q)
        qs.append(jnp.concatenate([jnp.where(first, q, zero), jnp.where(first, zero, q)], axis=0))

    nt = tq // LANES

    def tiles(s):
        return [s[:, t * LANES:(t + 1) * LANES] for t in range(nt)]

    def scores(hh, j):
        start = pl.multiple_of(j * tq, tq)
        s = _dot_nt(qs[hh], head(k_ref, hh, pl.ds(start, tq)))
        s_scr[hh, j] = s
        return functools.reduce(jnp.maximum, tiles(s))

    diag = pl.multiple_of(i * tq, tq)
    sub_row = lax.broadcasted_iota(jnp.int32, (2 * LANES, 1), 0) % LANES
    hidden = (sub_row < CHUNK) & (lane >= CHUNK)

    def sub_rows(u):
        return slice(u * LANES, (u + 1) * LANES), slice(tq + u * LANES, tq + (u + 1) * LANES)

    def diag_scores(hh):
        for u in range(nt):
            r1, r2 = sub_rows(u)
            nk = (u + 1) * LANES
            s = _dot_nt(jnp.concatenate([qs[hh][r1], qs[hh][r2]], axis=0),
                        head(k_ref, hh, pl.ds(diag, nk)))
            parts = [s[:, t * LANES:(t + 1) * LANES] for t in range(u)]
            parts.append(jnp.where(hidden, -jnp.inf, s[:, u * LANES:nk]))
            for t, part in enumerate(parts):
                s_scr[hh, i, r1, t * LANES:(t + 1) * LANES] = part[:LANES]
                s_scr[hh, i, r2, t * LANES:(t + 1) * LANES] = part[LANES:]
            m = functools.reduce(jnp.maximum, parts)
            m_scr[hh, r1] = m[:LANES]
            m_scr[hh, r2] = m[LANES:]

    for hh in heads:
        diag_scores(hh)

    n_pairs = lax.shift_right_logical(i, 1)
    leftover = jnp.bitwise_and(i, 1) == 1

    def pass1_block(j):
        for hh in heads:
            m_scr[hh] = jnp.maximum(m_scr[hh], scores(hh, j))

    def pass1(jj, carry):
        pass1_block(2 * jj)
        pass1_block(2 * jj + 1)
        return carry

    lax.fori_loop(0, n_pairs, pass1, 0)
    pl.when(leftover)(lambda: pass1_block(i - 1))
    mb = [jnp.broadcast_to(jnp.max(m_scr[hh], axis=-1, keepdims=True), (2 * tq, LANES))
          for hh in heads]

    def with_ones(v):
        return jnp.concatenate([v, jnp.ones_like(v)], axis=1)

    def probs_v(hh, j):
        start = pl.multiple_of(j * tq, tq)
        ps = [jnp.exp2(t - mb[hh]).astype(BF16) for t in tiles(s_scr[hh, j])]
        return _dot(jnp.concatenate(ps, axis=1), with_ones(head(v_ref, hh, pl.ds(start, tq))))

    def diag_probs_v(hh):
        for u in range(nt):
            r1, r2 = sub_rows(u)
            nk = (u + 1) * LANES
            mbu = jnp.concatenate([mb[hh][r1], mb[hh][r2]], axis=0)
            ps = [jnp.exp2(jnp.concatenate([s_scr[hh, i, r1, t * LANES:(t + 1) * LANES],
                                            s_scr[hh, i, r2, t * LANES:(t + 1) * LANES]], axis=0)
                           - mbu).astype(BF16) for t in range(u + 1)]
            pv = _dot(jnp.concatenate(ps, axis=1),
                      with_ones(head(v_ref, hh, pl.ds(diag, nk))))
            acc_scr[hh, r1], acc_scr[hh, r2] = pv[:LANES], pv[LANES:]

    for hh in heads:
        diag_probs_v(hh)

    def pass2_block(j):
        for hh in heads:
            acc_scr[hh] += probs_v(hh, j)

    def pass2(jj, carry):
        pass2_block(2 * jj)
        pass2_block(2 * jj + 1)
        return carry

    lax.fori_loop(0, n_pairs, pass2, 0)
    pl.when(leftover)(lambda: pass2_block(i - 1))

    for hh in heads:
        acc = acc_scr[hh]
        o = acc[:, :LANES] / acc[:, LANES:]
        o = o[:tq] - lam * o[tq:]
        ms = jnp.mean(o * o, axis=-1, keepdims=True)
        o = o * lax.rsqrt(ms + SUBLN_EPS) * sg_ref[...] * (1.0 - lam_init)
        g = head(g_ref, hh).astype(F32)
        o_ref[0, :, hh * LANES:(hh + 1) * LANES] = (
            o * (g / (1.0 + jnp.exp(-g)))).astype(o_ref.dtype)


def _attention(proj_a, lam_params, subln_g, bsz, seq, tq, lam_init):
    ng = ATTN_HEADS // ATTN_GROUP
    gw = ATTN_GROUP * LANES
    kernel = functools.partial(_attn_kernel, tq=tq, lam_init=lam_init)
    return pl.pallas_call(
        kernel,
        grid=(bsz, ng, seq // tq),
        in_specs=[
            pl.BlockSpec((4, ATTN_HEAD_DIM), lambda b, h, i: (0, 0)),
            pl.BlockSpec((1, tq, gw), lambda b, h, i: (b, i, h)),
            pl.BlockSpec((1, seq, gw), lambda b, h, i: (b, 0, ng + h)),
            pl.BlockSpec((1, seq, gw), lambda b, h, i: (b, 0, 2 * ng + h)),
            pl.BlockSpec((1, tq, gw), lambda b, h, i: (b, i, 3 * ng + h)),
            pl.BlockSpec((1, LANES), lambda b, h, i: (0, 0)),
        ],
        out_specs=pl.BlockSpec((1, tq, gw), lambda b, h, i: (b, i, h)),
        out_shape=jax.ShapeDtypeStruct((bsz, seq, W_ATTN), BF16),
        scratch_shapes=[pltpu.VMEM((ATTN_GROUP, seq // tq, 2 * tq, tq), F32),
                        pltpu.VMEM((ATTN_GROUP, 2 * tq, LANES), F32),
                        pltpu.VMEM((ATTN_GROUP, 2 * tq, 2 * LANES), F32)],
        compiler_params=pltpu.CompilerParams(
            dimension_semantics=("parallel", "parallel", "arbitrary"),
            vmem_limit_bytes=VMEM_LIMIT),
        name="diff_attn",
    )(lam_params, proj_a, proj_a, proj_a, proj_a, subln_g)


def _rwkv_kernel(pre_ref, lw_ref, lng_ref, lnb_ref, o_ref, state_ref, *, nchunk):
    c = CHUNK
    hd = RWKV_HEAD_DIM

    @pl.when(pl.program_id(1) == 0)
    def _():
        state_ref[...] = jnp.zeros_like(state_ref)

    pre = {name: pre_ref[idx, 0].astype(F32) for idx, name in enumerate(PRE_NAMES)}
    vb, bonus, gate = pre["vm"], pre["bonus"], pre["gate"]
    at_all, rt_all, bt_all, kt_all, bp_all, kp_all, pc_all = _decay_operands(
        pre, lw_ref[0], nchunk)

    lane = lax.broadcasted_iota(jnp.int32, (1, LANES), 1)
    first = lane < hd
    first2 = jnp.concatenate([first, first], axis=1)
    rowc = lax.broadcasted_iota(jnp.int32, (c, LANES), 0)
    colc = lax.broadcasted_iota(jnp.int32, (c, LANES), 1) % hd
    strict = colc < rowc
    incl = colc <= rowc
    r2 = lax.broadcasted_iota(jnp.int32, (LANES, LANES), 0)
    c2 = lax.broadcasted_iota(jnp.int32, (LANES, LANES), 1)
    same_head = (r2 // hd) == (c2 // hd)
    eye = r2 == c2

    def lower_blk(s):
        return ((r2 // (2 * s)) == (c2 // (2 * s))) & ((r2 // s) % 2 == 1) & ((c2 // s) % 2 == 0)
    zeros_cl = jnp.zeros((c, LANES), F32)

    def h0(x):
        return jnp.where(first, x, 0.0)

    def h1(x):
        return jnp.where(first, 0.0, x)

    units = [(k, p) for k in range(nchunk) for p in range(PAIRS)]

    def cut(x, k, p):
        return x[k * c:(k + 1) * c, p * LANES:(p + 1) * LANES]

    at = [cut(at_all, k, p) for k, p in units]
    rt = [cut(rt_all, k, p) for k, p in units]
    vm = [cut(vb, k, p) for k, p in units]

    mm = []
    for n, (k, p) in enumerate(units):
        bt, kt = cut(bt_all, k, p), cut(kt_all, k, p)
        ar = jnp.concatenate([at[n], rt[n]], axis=0).astype(BF16)
        rh = jnp.concatenate([h0(bt), h0(kt), h1(kt), h1(bt)], axis=0).astype(BF16)
        mm.append(_dot_nt(ar, rh))
    la0 = [jnp.where(strict, m[:c, :LANES], 0.0) for m in mm]
    la1 = [jnp.where(strict, m[:c, LANES:], 0.0) for m in mm]
    mr0 = [jnp.where(incl, m[c:, :LANES], 0.0) for m in mm]
    mr1 = [jnp.where(incl, m[c:, LANES:], 0.0) for m in mm]

    lakv = [_dot(jnp.where(first, la1[n], la0[n]).astype(BF16),
                 jnp.concatenate([h1(vm[n]), h0(vm[n])], axis=0).astype(BF16))
            for n in range(len(units))]

    lbd = [jnp.concatenate([h0(la0[n]), h1(la1[n])], axis=0) for n in range(len(units))]
    tinv = [(jnp.where(eye, 1.0, 0.0) + jnp.where(lower_blk(1), l, 0.0)).astype(BF16) for l in lbd]
    lbd16 = [l.astype(BF16) for l in lbd]
    sz = 2
    while sz < c:
        msk = lower_blk(sz)
        inner = [_dot(l, t).astype(BF16) for l, t in zip(lbd16, tinv)]
        tinv = [jnp.where(msk, _dot(t, m).astype(BF16), t) for t, m in zip(tinv, inner)]
        sz *= 2
    tz = []
    for n in range(len(units)):
        zc = jnp.concatenate([at[n], lakv[n]], axis=1)
        x = jnp.concatenate([jnp.where(first2, zc, 0.0), jnp.where(first2, 0.0, zc)], axis=0)
        xs2 = _dot(tinv[n], x.astype(BF16))
        tz.append(xs2[:c] + xs2[c:])
    w = [t[:, :LANES] for t in tz]
    u0 = [t[:, LANES:] for t in tz]

    gmat, hmat = [], []
    for n, (k, p) in enumerate(units):
        bp, kp = cut(bp_all, k, p), cut(kp_all, k, p)
        lt = jnp.concatenate([bp, kp], axis=0).T.astype(BF16)
        rt2 = jnp.concatenate([jnp.concatenate([w[n], u0[n]], axis=1),
                               jnp.concatenate([zeros_cl, vm[n]], axis=1)], axis=0).astype(BF16)
        gh = _dot(lt, rt2)
        pc = pc_all[k][:, p * LANES:(p + 1) * LANES]
        gmat.append(jnp.where(same_head, gh[:, :LANES], 0.0) + jnp.where(eye, pc, 0.0))
        hmat.append(jnp.where(same_head, gh[:, LANES:], 0.0))

    qh, y0 = [], []
    for n in range(len(units)):
        lq = jnp.concatenate([jnp.where(first, mr0[n], mr1[n]), jnp.where(first, mr1[n], mr0[n])],
                             axis=1).astype(BF16)
        rq = jnp.concatenate([
            jnp.concatenate([h0(w[n]), h0(u0[n])], axis=1),
            jnp.concatenate([h1(w[n]), h1(u0[n])], axis=1),
            jnp.concatenate([zeros_cl, h1(vm[n])], axis=1),
            jnp.concatenate([zeros_cl, h0(vm[n])], axis=1)], axis=0).astype(BF16)
        qy = _dot(lq, rq)
        qh.append(rt[n] + qy[:, :LANES])
        y0.append(qy[:, LANES:])

    state = [state_ref[p] for p in range(PAIRS)]
    ys = []
    for n, (k, p) in enumerate(units):
        gs = _dot(jnp.concatenate([gmat[n], qh[n]], axis=0).astype(BF16), state[p].astype(BF16))
        state[p] = gs[:LANES] + hmat[n]
        ys.append(gs[LANES:] + y0[n])
    for p in range(PAIRS):
        state_ref[p] = state[p]

    def pair_mean(t):
        s0 = jnp.sum(h0(t), axis=-1, keepdims=True)
        s1 = jnp.sum(h1(t), axis=-1, keepdims=True)
        return jnp.where(first, s0, s1) * (1.0 / hd)

    for n, (k, p) in enumerate(units):
        sl = slice(p * LANES, (p + 1) * LANES)
        y = ys[n]
        mu = pair_mean(y)
        d = y - mu
        var = pair_mean(d * d)
        yn = d * lax.rsqrt(var + GN_EPS) * lng_ref[:, sl] + lnb_ref[:, sl]
        o_ref[0, k * c:(k + 1) * c, sl] = (
            (yn + cut(bonus, k, p)) * cut(gate, k, p)).astype(o_ref.dtype)


def _rwkv(pre, logw, ln_g, ln_b, bsz, seq, nchunk):
    tt = nchunk * CHUNK
    vec = lambda n: pl.BlockSpec((1, n), lambda b, t: (0, 0))
    return pl.pallas_call(
        functools.partial(_rwkv_kernel, nchunk=nchunk),
        grid=(bsz, seq // tt),
        in_specs=[
            pl.BlockSpec((len(PRE_NAMES), 1, tt, W_RWKV), lambda b, t: (0, b, t, 0)),
            pl.BlockSpec((1, tt, W_RWKV), lambda b, t: (b, t, 0)),
            vec(W_RWKV), vec(W_RWKV),
        ],
        out_specs=pl.BlockSpec((1, tt, W_RWKV), lambda b, t: (b, t, 0)),
        out_shape=jax.ShapeDtypeStruct((bsz, seq, W_RWKV), BF16),
        scratch_shapes=[pltpu.VMEM((PAIRS, LANES, LANES), F32)],
        compiler_params=pltpu.CompilerParams(
            dimension_semantics=("parallel", "arbitrary"), vmem_limit_bytes=VMEM_LIMIT),
        name="rwkv7",
    )(pre, logw, ln_g, ln_b)


def _outproj_kernel(x_ref, oa_ref, ob_ref, wa_ref, wb_ref, fg_ref, o_ref, *, final):
    h = x_ref[...] + _dot(oa_ref[...], wa_ref[...]) + _dot(ob_ref[...], wb_ref[...])
    if final:
        ms = jnp.mean(h * h, axis=-1, keepdims=True)
        h = h * lax.rsqrt(ms + RMS_EPS) * fg_ref[...]
    o_ref[...] = h


def _outproj(x2d, oa, ob, w_oa, w_ob, final_g, tm, final):
    m = x2d.shape[0]
    return pl.pallas_call(
        functools.partial(_outproj_kernel, final=final),
        grid=(m // tm,),
        in_specs=[
            pl.BlockSpec((tm, D_MODEL), lambda i: (i, 0)),
            pl.BlockSpec((tm, W_ATTN), lambda i: (i, 0)),
            pl.BlockSpec((tm, W_RWKV), lambda i: (i, 0)),
            pl.BlockSpec((W_ATTN, D_MODEL), lambda i: (0, 0)),
            pl.BlockSpec((W_RWKV, D_MODEL), lambda i: (0, 0)),
            pl.BlockSpec((1, D_MODEL), lambda i: (0, 0)),
        ],
        out_specs=pl.BlockSpec((tm, D_MODEL), lambda i: (i, 0)),
        out_shape=jax.ShapeDtypeStruct((m, D_MODEL), F32),
        compiler_params=pltpu.CompilerParams(
            dimension_semantics=("parallel",), vmem_limit_bytes=VMEM_LIMIT),
        name="outproj",
    )(x2d, oa, ob, w_oa, w_ob, final_g)


def _pad_rows(w, top):
    zeros = jnp.zeros_like(w)
    return jnp.concatenate([w, zeros] if top else [zeros, w], axis=0)


def kernel(x, norm_g, w_in, lambda_q1, lambda_k1, lambda_q2, lambda_k2, subln_g,
           shift_mu, w0, w2, a0, a2, k_k, k_a, r_k, ln_x_g, ln_x_b, w_out, final_g):
    bsz, seq, d = x.shape
    depth = w_in.shape[0]
    assert d == D_MODEL and w_in.shape[2] == D_IN
    m = bsz * seq
    tm_in = min(1024, seq)
    tm_out = min(2048, m)
    tq = min(512, seq)
    nchunk = 4
    assert seq % tm_in == 0 and tm_in % PREP_ROWS == 0 and m % tm_out == 0
    assert seq % tq == 0 and seq % (nchunk * CHUNK) == 0

    row = lambda v: v.reshape(1, -1).astype(F32)
    h = x.reshape(m, d)
    for l in range(depth):
        proj_a, pre, logw = _inproj(
            h, row(norm_g[l]), w_in[l, :, :A_COLS].astype(BF16), w_in[l, :, A_COLS:].astype(BF16),
            row(shift_mu[l]), row(w0[l]), _pad_rows(w2[l], True).astype(BF16), row(a0[l]),
            _pad_rows(a2[l], False).astype(BF16), row(k_k[l]), row(k_a[l]), row(r_k[l]),
            tm_in, seq)
        lam_params = jnp.stack([lambda_q1[l], lambda_k1[l], lambda_q2[l], lambda_k2[l]]).astype(F32)
        o_a = _attention(proj_a.reshape(bsz, seq, A_COLS), lam_params, row(subln_g[l]),
                         bsz, seq, tq, _lambda_init(l))
        o_b = _rwkv(pre.reshape(len(PRE_NAMES), bsz, seq, W_RWKV),
                    logw.reshape(bsz, seq, W_RWKV),
                    row(ln_x_g[l]), row(ln_x_b[l]), bsz, seq, nchunk)
        h = _outproj(h, o_a.reshape(m, W_ATTN), o_b.reshape(m, W_RWKV),
                     w_out[l, :W_ATTN].astype(BF16), w_out[l, W_ATTN:].astype(BF16),
                     row(final_g), tm_out, final=(l == depth - 1))
    return h.reshape(bsz, seq, d)
```

```python
import functools
import math

import jax
import jax.numpy as jnp
from jax import lax
from jax.experimental import pallas as pl
from jax.experimental.pallas import tpu as pltpu

F32 = jnp.float32
BF16 = jnp.bfloat16

D_MODEL = 1024
CHUNK = 64
W_ATTN = D_MODEL // 2
W_RWKV = D_MODEL - W_ATTN
ATTN_HEAD_DIM = 64
ATTN_HEADS = W_ATTN // (2 * ATTN_HEAD_DIM)
RWKV_HEAD_DIM = 64
LORA = 64
RMS_EPS = 1e-6
SUBLN_EPS = 1e-5
GN_EPS = 1e-5 * RWKV_HEAD_DIM
A_COLS = 4 * W_ATTN
SHIFT_WIDTH = 3 * W_RWKV + 2 * LORA
B_COLS = SHIFT_WIDTH + W_RWKV
D_IN = A_COLS + B_COLS

LANES = 128
PAIRS = W_RWKV // LANES
VMEM_LIMIT = 56 * 1024 * 1024


def _lambda_init(layer_idx):
    return 0.8 - 0.6 * math.exp(-0.3 * layer_idx)


def _dot(a, b):
    return jnp.dot(a, b, preferred_element_type=F32)


def _dot_nt(a, b):
    return lax.dot_general(a, b, (((1,), (1,)), ((), ())), preferred_element_type=F32)


PRE_NAMES = ("r", "kk", "kka", "km", "vm", "bonus", "gate")
PREP_ROWS = 256


def _split3(x):
    hi = x.astype(BF16)
    r1 = x - hi.astype(F32)
    mid = r1.astype(BF16)
    lo = (r1 - mid.astype(F32)).astype(BF16)
    return hi, mid, lo


def _rwkv_prepare(z, prev_row, mu_ref, w0_ref, w2_ref, a0_ref, a2_ref, kk_ref, ka_ref, rk_ref):
    hd = RWKV_HEAD_DIM
    rows = z.shape[0]
    zs = z[:, :SHIFT_WIDTH]
    row = lax.broadcasted_iota(jnp.int32, (rows, 1), 0)
    prev = jnp.where(row == 0, prev_row, pltpu.roll(zs, 1, axis=0))
    xs = zs + (prev - zs) * mu_ref[...]

    r = xs[:, 0:W_RWKV]
    kb = xs[:, W_RWKV:2 * W_RWKV]
    vb = xs[:, 2 * W_RWKV:3 * W_RWKV]
    lora_in = xs[:, 3 * W_RWKV:SHIFT_WIDTH]
    g = z[:, SHIFT_WIDTH:]

    u = w0_ref[...] + _dot(jnp.tanh(lora_in).astype(BF16), w2_ref[...])
    w_log = -(jnp.maximum(-u, 0.0) + jnp.log(1.0 + jnp.exp(-jnp.abs(u)))) - 0.5
    logw = -jnp.exp(w_log)
    a = 1.0 / (1.0 + jnp.exp(-(a0_ref[...] + _dot(lora_in.astype(BF16), a2_ref[...]))))

    lane4 = lax.broadcasted_iota(jnp.int32, (1, W_RWKV), 1)
    first4 = (lane4 % LANES) < hd

    def head_sum(x):
        outs = []
        for p in range(PAIRS):
            xp = x[:, p * LANES:(p + 1) * LANES]
            fp = first4[:, p * LANES:(p + 1) * LANES]
            s0 = jnp.sum(jnp.where(fp, xp, 0.0), axis=-1, keepdims=True)
            s1 = jnp.sum(jnp.where(fp, 0.0, xp), axis=-1, keepdims=True)
            outs.append(jnp.where(fp, s0, s1))
        return jnp.concatenate(outs, axis=1)

    kk = kb * kk_ref[...]
    kk = kk * lax.rsqrt(jnp.maximum(head_sum(kk * kk), 1e-24))
    k_mod = kb * (1.0 + (a - 1.0) * ka_ref[...])
    bonus = head_sum(r * k_mod * rk_ref[...]) * vb
    pre = dict(r=r, kk=kk, kka=kk * a, km=k_mod, vm=vb, bonus=bonus,
               gate=g / (1.0 + jnp.exp(-g)))
    return pre, logw


def _decay_operands(pre, logw, nchunk):
    c = CHUNK
    rows = nchunk * c
    ri = lax.broadcasted_iota(jnp.int32, (rows, rows), 0)
    ci = lax.broadcasted_iota(jnp.int32, (rows, rows), 1)
    tri = ((ci <= ri) & ((ci // c) == (ri // c))).astype(BF16)
    hi, mid, lo = _split3(logw)
    log2e = math.log2(math.e)
    cs = (_dot(tri, hi) + _dot(tri, mid) + _dot(tri, lo)) * log2e
    c_ends = [cs[k * c + c - 1:k * c + c, :] for k in range(nchunk)]
    c_end = jnp.concatenate([jnp.broadcast_to(ce, (c, W_RWKV)) for ce in c_ends], axis=0)
    e_neg = jnp.exp2(-cs)
    e_end = jnp.exp2(c_end - cs)
    at = -pre["kk"] * jnp.exp2(cs - logw * log2e)
    rt = pre["r"] * jnp.exp2(cs)
    return (at, rt, pre["kka"] * e_neg, pre["km"] * e_neg, pre["kka"] * e_end, pre["km"] * e_end,
            [jnp.exp2(ce) for ce in c_ends])


def _inproj_kernel(x_ref, g_ref, wa_ref, wb_ref, mu_ref, w0_ref, w2_ref, a0_ref, a2_ref, kk_ref,
                   ka_ref, rk_ref, oa_ref, pre_ref, lw_ref, prev_scr, *, steps_per_seq):
    tm = x_ref.shape[0]
    q_scale = ATTN_HEAD_DIM ** -0.5 * math.log2(math.e)

    @pl.when(pl.program_id(0) % steps_per_seq == 0)
    def _():
        prev_scr[...] = jnp.zeros_like(prev_scr)

    def normed(n):
        x = x_ref[n * PREP_ROWS:(n + 1) * PREP_ROWS, :]
        ms = jnp.mean(x * x, axis=-1, keepdims=True)
        return (x * lax.rsqrt(ms + RMS_EPS) * g_ref[...]).astype(BF16)

    def project_attn(n, xn):
        rows = slice(n * PREP_ROWS, (n + 1) * PREP_ROWS)
        oa_ref[rows, :W_ATTN] = (_dot(xn, wa_ref[:, :W_ATTN]) * q_scale).astype(oa_ref.dtype)
        oa_ref[rows, W_ATTN:] = _dot(xn, wa_ref[:, W_ATTN:]).astype(oa_ref.dtype)

    def prepare(n, z, prev_row):
        rows = slice(n * PREP_ROWS, (n + 1) * PREP_ROWS)
        pre, logw = _rwkv_prepare(z, prev_row, mu_ref, w0_ref, w2_ref, a0_ref, a2_ref, kk_ref,
                                  ka_ref, rk_ref)
        for idx, name in enumerate(PRE_NAMES):
            pre_ref[idx, rows, :] = pre[name].astype(pre_ref.dtype)
        lw_ref[rows, :] = logw

    nsub = tm // PREP_ROWS
    prev_row = prev_scr[...]
    xn = normed(0)
    z = _dot(xn, wb_ref[...])
    for n in range(nsub):
        if n + 1 < nsub:
            xn_next = normed(n + 1)
            z_next = _dot(xn_next, wb_ref[...])
        prepare(n, z, prev_row)
        project_attn(n, xn)
        prev_row = z[PREP_ROWS - 1:PREP_ROWS, :SHIFT_WIDTH]
        if n + 1 < nsub:
            xn, z = xn_next, z_next
    prev_scr[...] = prev_row


def _inproj(x2d, norm_g, w_a, w_b, mu, w0, w2p, a0, a2p, k_k, k_a, r_k, tm, seq):
    m = x2d.shape[0]
    vec = lambda n: pl.BlockSpec((1, n), lambda i: (0, 0))
    mat = pl.BlockSpec((LANES, W_RWKV), lambda i: (0, 0))
    return pl.pallas_call(
        functools.partial(_inproj_kernel, steps_per_seq=seq // tm),
        grid=(m // tm,),
        in_specs=[
            pl.BlockSpec((tm, D_MODEL), lambda i: (i, 0)),
            vec(D_MODEL),
            pl.BlockSpec((D_MODEL, A_COLS), lambda i: (0, 0)),
            pl.BlockSpec((D_MODEL, B_COLS), lambda i: (0, 0)),
            vec(SHIFT_WIDTH), vec(W_RWKV), mat, vec(W_RWKV), mat,
            vec(W_RWKV), vec(W_RWKV), vec(W_RWKV),
        ],
        out_specs=[
            pl.BlockSpec((tm, A_COLS), lambda i: (i, 0)),
            pl.BlockSpec((len(PRE_NAMES), tm, W_RWKV), lambda i: (0, i, 0)),
            pl.BlockSpec((tm, W_RWKV), lambda i: (i, 0)),
        ],
        out_shape=[
            jax.ShapeDtypeStruct((m, A_COLS), BF16),
            jax.ShapeDtypeStruct((len(PRE_NAMES), m, W_RWKV), BF16),
            jax.ShapeDtypeStruct((m, W_RWKV), F32),
        ],
        scratch_shapes=[pltpu.VMEM((1, SHIFT_WIDTH), F32)],
        compiler_params=pltpu.CompilerParams(
            dimension_semantics=("arbitrary",), vmem_limit_bytes=VMEM_LIMIT),
        name="inproj",
    )(x2d, norm_g, w_a, w_b, mu, w0, w2p, a0, a2p, k_k, k_a, r_k)


ATTN_GROUP = 2


def _attn_kernel(lam_ref, q_ref, k_ref, v_ref, g_ref, sg_ref, o_ref, s_scr, m_scr, acc_scr,
                 *, tq, lam_init):
    i = pl.program_id(2)
    lane = lax.broadcasted_iota(jnp.int32, (1, LANES), 1)
    first = lane < ATTN_HEAD_DIM
    heads = range(ATTN_GROUP)

    lp = lam_ref[...]
    l1 = jnp.sum(lp[0:1] * lp[1:2], axis=-1, keepdims=True)
    l2 = jnp.sum(lp[2:3] * lp[3:4], axis=-1, keepdims=True)
    lam = jnp.exp(l1) - jnp.exp(l2) + lam_init

    def head(ref, hh, rows=slice(None)):
        return ref[0, rows, hh * LANES:(hh + 1) * LANES]

    qs = []
    for hh in heads:
        q = head(q_ref, hh)
        zero = jnp.zeros_like(q)
        qs.append(jnp.concatenate([jnp.where(first, q, zero), jnp.where(first, zero, q)], axis=0))

    nt = tq // LANES

    def tiles(s):
        return [s[:, t * LANES:(t + 1) * LANES] for t in range(nt)]

    def scores(hh, j):
        start = pl.multiple_of(j * tq, tq)
        s = _dot_nt(qs[hh], head(k_ref, hh, pl.ds(start, tq)))
        s_scr[hh, j] = s
        return functools.reduce(jnp.maximum, tiles(s))

    diag = pl.multiple_of(i * tq, tq)
    sub_row = lax.broadcasted_iota(jnp.int32, (2 * LANES, 1), 0) % LANES
    hidden = (sub_row < CHUNK) & (lane >= CHUNK)

    def sub_rows(u):
        return slice(u * LANES, (u + 1) * LANES), slice(tq + u * LANES, tq + (u + 1) * LANES)

    def diag_scores(hh):
        for u in range(nt):
            r1, r2 = sub_rows(u)
            nk = (u + 1) * LANES
            s = _dot_nt(jnp.concatenate([qs[hh][r1], qs[hh][r2]], axis=0),
                        head(k_ref, hh, pl.ds(diag, nk)))
            parts = [s[:, t * LANES:(t + 1) * LANES] for t in range(u)]
            parts.append(jnp.where(hidden, -jnp.inf, s[:, u * LANES:nk]))
            for t, part in enumerate(parts):
                s_scr[hh, i, r1, t * LANES:(t + 1) * LANES] = part[:LANES]
                s_scr[hh, i, r2, t * LANES:(t + 1) * LANES] = part[LANES:]
            m = functools.reduce(jnp.maximum, parts)
            m_scr[hh, r1] = m[:LANES]
            m_scr[hh, r2] = m[LANES:]

    for hh in heads:
        diag_scores(hh)

    def sweep(block_fn):
        n_quads = lax.shift_right_logical(i, 2)

        def body(jj, carry):
            for d in range(4):
                block_fn(4 * jj + d)
            return carry

        lax.fori_loop(0, n_quads, body, 0)
        base = 4 * n_quads

        @pl.when(jnp.bitwise_and(i, 2) == 2)
        def _():
            block_fn(base)
            block_fn(base + 1)

        @pl.when(jnp.bitwise_and(i, 1) == 1)
        def _():
            block_fn(i - 1)

    def pass1_block(j):
        for hh in heads:
            m_scr[hh] = jnp.maximum(m_scr[hh], scores(hh, j))

    sweep(pass1_block)
    mb = [jnp.broadcast_to(jnp.max(m_scr[hh], axis=-1, keepdims=True), (2 * tq, LANES))
          for hh in heads]

    def with_ones(v):
        return jnp.concatenate([v, jnp.ones_like(v)], axis=1)

    def probs_v(hh, j):
        start = pl.multiple_of(j * tq, tq)
        ps = [jnp.exp2(t - mb[hh]).astype(BF16) for t in tiles(s_scr[hh, j])]
        return _dot(jnp.concatenate(ps, axis=1), with_ones(head(v_ref, hh, pl.ds(start, tq))))

    def diag_probs_v(hh):
        for u in range(nt):
            r1, r2 = sub_rows(u)
            nk = (u + 1) * LANES
            mbu = jnp.concatenate([mb[hh][r1], mb[hh][r2]], axis=0)
            ps = [jnp.exp2(jnp.concatenate([s_scr[hh, i, r1, t * LANES:(t + 1) * LANES],
                                            s_scr[hh, i, r2, t * LANES:(t + 1) * LANES]], axis=0)
                           - mbu).astype(BF16) for t in range(u + 1)]
            pv = _dot(jnp.concatenate(ps, axis=1),
                      with_ones(head(v_ref, hh, pl.ds(diag, nk))))
            acc_scr[hh, r1], acc_scr[hh, r2] = pv[:LANES], pv[LANES:]

    for hh in heads:
        diag_probs_v(hh)

    def pass2_block(j):
        for hh in heads:
            acc_scr[hh] += probs_v(hh, j)

    sweep(pass2_block)

    for hh in heads:
        acc = acc_scr[hh]
        o = acc[:, :LANES] / acc[:, LANES:]
        o = o[:tq] - lam * o[tq:]
        ms = jnp.mean(o * o, axis=-1, keepdims=True)
        o = o * lax.rsqrt(ms + SUBLN_EPS) * sg_ref[...] * (1.0 - lam_init)
        g = head(g_ref, hh).astype(F32)
        o_ref[0, :, hh * LANES:(hh + 1) * LANES] = (
            o * (g / (1.0 + jnp.exp(-g)))).astype(o_ref.dtype)


def _attention(proj_a, lam_params, subln_g, bsz, seq, tq, lam_init):
    ng = ATTN_HEADS // ATTN_GROUP
    gw = ATTN_GROUP * LANES
    kernel = functools.partial(_attn_kernel, tq=tq, lam_init=lam_init)
    return pl.pallas_call(
        kernel,
        grid=(bsz, ng, seq // tq),
        in_specs=[
            pl.BlockSpec((4, ATTN_HEAD_DIM), lambda b, h, i: (0, 0)),
            pl.BlockSpec((1, tq, gw), lambda b, h, i: (b, i, h)),
            pl.BlockSpec((1, seq, gw), lambda b, h, i: (b, 0, ng + h)),
            pl.BlockSpec((1, seq, gw), lambda b, h, i: (b, 0, 2 * ng + h)),
            pl.BlockSpec((1, tq, gw), lambda b, h, i: (b, i, 3 * ng + h)),
            pl.BlockSpec((1, LANES), lambda b, h, i: (0, 0)),
        ],
        out_specs=pl.BlockSpec((1, tq, gw), lambda b, h, i: (b, i, h)),
        out_shape=jax.ShapeDtypeStruct((bsz, seq, W_ATTN), BF16),
        scratch_shapes=[pltpu.VMEM((ATTN_GROUP, seq // tq, 2 * tq, tq), F32),
                        pltpu.VMEM((ATTN_GROUP, 2 * tq, LANES), F32),
                        pltpu.VMEM((ATTN_GROUP, 2 * tq, 2 * LANES), F32)],
        compiler_params=pltpu.CompilerParams(
            dimension_semantics=("parallel", "parallel", "arbitrary"),
            vmem_limit_bytes=VMEM_LIMIT),
        name="diff_attn",
    )(lam_params, proj_a, proj_a, proj_a, proj_a, subln_g)


def _rwkv_kernel(pre_ref, lw_ref, lng_ref, lnb_ref, o_ref, state_ref, *, nchunk):
    c = CHUNK
    hd = RWKV_HEAD_DIM

    @pl.when(pl.program_id(1) == 0)
    def _():
        state_ref[...] = jnp.zeros_like(state_ref)

    pre = {name: pre_ref[idx, 0].astype(F32) for idx, name in enumerate(PRE_NAMES)}
    vb, bonus, gate = pre["vm"], pre["bonus"], pre["gate"]
    at_all, rt_all, bt_all, kt_all, bp_all, kp_all, pc_all = _decay_operands(
        pre, lw_ref[0], nchunk)

    lane = lax.broadcasted_iota(jnp.int32, (1, LANES), 1)
    first = lane < hd
    first2 = jnp.concatenate([first, first], axis=1)
    rowc = lax.broadcasted_iota(jnp.int32, (c, LANES), 0)
    colc = lax.broadcasted_iota(jnp.int32, (c, LANES), 1) % hd
    strict = colc < rowc
    incl = colc <= rowc
    r2 = lax.broadcasted_iota(jnp.int32, (LANES, LANES), 0)
    c2 = lax.broadcasted_iota(jnp.int32, (LANES, LANES), 1)
    same_head = (r2 // hd) == (c2 // hd)
    eye = r2 == c2

    def lower_blk(s):
        return ((r2 // (2 * s)) == (c2 // (2 * s))) & ((r2 // s) % 2 == 1) & ((c2 // s) % 2 == 0)
    zeros_cl = jnp.zeros((c, LANES), F32)

    def h0(x):
        return jnp.where(first, x, 0.0)

    def h1(x):
        return jnp.where(first, 0.0, x)

    units = [(k, p) for k in range(nchunk) for p in range(PAIRS)]

    def cut(x, k, p):
        return x[k * c:(k + 1) * c, p * LANES:(p + 1) * LANES]

    at = [cut(at_all, k, p) for k, p in units]
    rt = [cut(rt_all, k, p) for k, p in units]
    vm = [cut(vb, k, p) for k, p in units]

    mm = []
    for n, (k, p) in enumerate(units):
        bt, kt = cut(bt_all, k, p), cut(kt_all, k, p)
        ar = jnp.concatenate([at[n], rt[n]], axis=0).astype(BF16)
        rh = jnp.concatenate([h0(bt), h0(kt), h1(kt), h1(bt)], axis=0).astype(BF16)
        mm.append(_dot_nt(ar, rh))
    la0 = [jnp.where(strict, m[:c, :LANES], 0.0) for m in mm]
    la1 = [jnp.where(strict, m[:c, LANES:], 0.0) for m in mm]
    mr0 = [jnp.where(incl, m[c:, :LANES], 0.0) for m in mm]
    mr1 = [jnp.where(incl, m[c:, LANES:], 0.0) for m in mm]

    lakv = [_dot(jnp.where(first, la1[n], la0[n]).astype(BF16),
                 jnp.concatenate([h1(vm[n]), h0(vm[n])], axis=0).astype(BF16))
            for n in range(len(units))]

    lbd = [jnp.concatenate([h0(la0[n]), h1(la1[n])], axis=0) for n in range(len(units))]
    tinv = [(jnp.where(eye, 1.0, 0.0) + jnp.where(lower_blk(1), l, 0.0)).astype(BF16) for l in lbd]
    lbd16 = [l.astype(BF16) for l in lbd]
    sz = 2
    while sz < c:
        msk = lower_blk(sz)
        inner = [_dot(l, t).astype(BF16) for l, t in zip(lbd16, tinv)]
        tinv = [jnp.where(msk, _dot(t, m).astype(BF16), t) for t, m in zip(tinv, inner)]
        sz *= 2
    tz = []
    for n in range(len(units)):
        zc = jnp.concatenate([at[n], lakv[n]], axis=1)
        x = jnp.concatenate([jnp.where(first2, zc, 0.0), jnp.where(first2, 0.0, zc)], axis=0)
        xs2 = _dot(tinv[n], x.astype(BF16))
        tz.append(xs2[:c] + xs2[c:])
    w = [t[:, :LANES] for t in tz]
    u0 = [t[:, LANES:] for t in tz]

    gmat, hmat = [], []
    for n, (k, p) in enumerate(units):
        bp, kp = cut(bp_all, k, p), cut(kp_all, k, p)
        lt = jnp.concatenate([bp, kp], axis=0).T.astype(BF16)
        rt2 = jnp.concatenate([jnp.concatenate([w[n], u0[n]], axis=1),
                               jnp.concatenate([zeros_cl, vm[n]], axis=1)], axis=0).astype(BF16)
        gh = _dot(lt, rt2)
        pc = pc_all[k][:, p * LANES:(p + 1) * LANES]
        gmat.append(jnp.where(same_head, gh[:, :LANES], 0.0) + jnp.where(eye, pc, 0.0))
        hmat.append(jnp.where(same_head, gh[:, LANES:], 0.0))

    qh, y0 = [], []
    for n in range(len(units)):
        lq = jnp.concatenate([jnp.where(first, mr0[n], mr1[n]), jnp.where(first, mr1[n], mr0[n])],
                             axis=1).astype(BF16)
        rq = jnp.concatenate([
            jnp.concatenate([h0(w[n]), h0(u0[n])], axis=1),
            jnp.concatenate([h1(w[n]), h1(u0[n])], axis=1),
            jnp.concatenate([zeros_cl, h1(vm[n])], axis=1),
            jnp.concatenate([zeros_cl, h0(vm[n])], axis=1)], axis=0).astype(BF16)
        qy = _dot(lq, rq)
        qh.append(rt[n] + qy[:, :LANES])
        y0.append(qy[:, LANES:])

    state = [state_ref[p] for p in range(PAIRS)]
    ys = []
    for n, (k, p) in enumerate(units):
        gs = _dot(jnp.concatenate([gmat[n], qh[n]], axis=0).astype(BF16), state[p].astype(BF16))
        state[p] = gs[:LANES] + hmat[n]
        ys.append(gs[LANES:] + y0[n])
    for p in range(PAIRS):
        state_ref[p] = state[p]

    def pair_mean(t):
        s0 = jnp.sum(h0(t), axis=-1, keepdims=True)
        s1 = jnp.sum(h1(t), axis=-1, keepdims=True)
        return jnp.where(first, s0, s1) * (1.0 / hd)

    for n, (k, p) in enumerate(units):
        sl = slice(p * LANES, (p + 1) * LANES)
        y = ys[n]
        mu = pair_mean(y)
        d = y - mu
        var = pair_mean(d * d)
        yn = d * lax.rsqrt(var + GN_EPS) * lng_ref[:, sl] + lnb_ref[:, sl]
        o_ref[0, k * c:(k + 1) * c, sl] = (
            (yn + cut(bonus, k, p)) * cut(gate, k, p)).astype(o_ref.dtype)


def _rwkv(pre, logw, ln_g, ln_b, bsz, seq, nchunk):
    tt = nchunk * CHUNK
    vec = lambda n: pl.BlockSpec((1, n), lambda b, t: (0, 0))
    return pl.pallas_call(
        functools.partial(_rwkv_kernel, nchunk=nchunk),
        grid=(bsz, seq // tt),
        in_specs=[
            pl.BlockSpec((len(PRE_NAMES), 1, tt, W_RWKV), lambda b, t: (0, b, t, 0)),
            pl.BlockSpec((1, tt, W_RWKV), lambda b, t: (b, t, 0)),
            vec(W_RWKV), vec(W_RWKV),
        ],
        out_specs=pl.BlockSpec((1, tt, W_RWKV), lambda b, t: (b, t, 0)),
        out_shape=jax.ShapeDtypeStruct((bsz, seq, W_RWKV), BF16),
        scratch_shapes=[pltpu.VMEM((PAIRS, LANES, LANES), F32)],
        compiler_params=pltpu.CompilerParams(
            dimension_semantics=("parallel", "arbitrary"), vmem_limit_bytes=VMEM_LIMIT),
        name="rwkv7",
    )(pre, logw, ln_g, ln_b)


def _outproj_kernel(x_ref, oa_ref, ob_ref, wa_ref, wb_ref, fg_ref, o_ref, *, final):
    h = x_ref[...] + _dot(oa_ref[...], wa_ref[...]) + _dot(ob_ref[...], wb_ref[...])
    if final:
        ms = jnp.mean(h * h, axis=-1, keepdims=True)
        h = h * lax.rsqrt(ms + RMS_EPS) * fg_ref[...]
    o_ref[...] = h


def _outproj(x2d, oa, ob, w_oa, w_ob, final_g, tm, final):
    m = x2d.shape[0]
    return pl.pallas_call(
        functools.partial(_outproj_kernel, final=final),
        grid=(m // tm,),
        in_specs=[
            pl.BlockSpec((tm, D_MODEL), lambda i: (i, 0)),
            pl.BlockSpec((tm, W_ATTN), lambda i: (i, 0)),
            pl.BlockSpec((tm, W_RWKV), lambda i: (i, 0)),
            pl.BlockSpec((W_ATTN, D_MODEL), lambda i: (0, 0)),
            pl.BlockSpec((W_RWKV, D_MODEL), lambda i: (0, 0)),
            pl.BlockSpec((1, D_MODEL), lambda i: (0, 0)),
        ],
        out_specs=pl.BlockSpec((tm, D_MODEL), lambda i: (i, 0)),
        out_shape=jax.ShapeDtypeStruct((m, D_MODEL), F32),
        compiler_params=pltpu.CompilerParams(
            dimension_semantics=("parallel",), vmem_limit_bytes=VMEM_LIMIT),
        name="outproj",
    )(x2d, oa, ob, w_oa, w_ob, final_g)


def _pad_rows(w, top):
    zeros = jnp.zeros_like(w)
    return jnp.concatenate([w, zeros] if top else [zeros, w], axis=0)


def kernel(x, norm_g, w_in, lambda_q1, lambda_k1, lambda_q2, lambda_k2, subln_g,
           shift_mu, w0, w2, a0, a2, k_k, k_a, r_k, ln_x_g, ln_x_b, w_out, final_g):
    bsz, seq, d = x.shape
    depth = w_in.shape[0]
    assert d == D_MODEL and w_in.shape[2] == D_IN
    m = bsz * seq
    tm_in = min(1024, seq)
    tm_out = min(2048, m)
    tq = min(512, seq)
    nchunk = 4
    assert seq % tm_in == 0 and tm_in % PREP_ROWS == 0 and m % tm_out == 0
    assert seq % tq == 0 and seq % (nchunk * CHUNK) == 0

    row = lambda v: v.reshape(1, -1).astype(F32)
    h = x.reshape(m, d)
    for l in range(depth):
        proj_a, pre, logw = _inproj(
            h, row(norm_g[l]), w_in[l, :, :A_COLS].astype(BF16), w_in[l, :, A_COLS:].astype(BF16),
            row(shift_mu[l]), row(w0[l]), _pad_rows(w2[l], True).astype(BF16), row(a0[l]),
            _pad_rows(a2[l], False).astype(BF16), row(k_k[l]), row(k_a[l]), row(r_k[l]),
            tm_in, seq)
        lam_params = jnp.stack([lambda_q1[l], lambda_k1[l], lambda_q2[l], lambda_k2[l]]).astype(F32)
        o_a = _attention(proj_a.reshape(bsz, seq, A_COLS), lam_params, row(subln_g[l]),
                         bsz, seq, tq, _lambda_init(l))
        o_b = _rwkv(pre.reshape(len(PRE_NAMES), bsz, seq, W_RWKV),
                    logw.reshape(bsz, seq, W_RWKV),
                    row(ln_x_g[l]), row(ln_x_b[l]), bsz, seq, nchunk)
        h = _outproj(h, o_a.reshape(m, W_ATTN), o_b.reshape(m, W_RWKV),
                     w_out[l, :W_ATTN].astype(BF16), w_out[l, W_ATTN:].astype(BF16),
                     row(final_g), tm_out, final=(l == depth - 1))
    return h.reshape(bsz, seq, d)
```

```python
import functools
import math

import jax
import jax.numpy as jnp
from jax import lax
from jax.experimental import pallas as pl
from jax.experimental.pallas import tpu as pltpu

F32 = jnp.float32
BF16 = jnp.bfloat16

D_MODEL = 1024
CHUNK = 64
W_ATTN = D_MODEL // 2
W_RWKV = D_MODEL - W_ATTN
ATTN_HEAD_DIM = 64
ATTN_HEADS = W_ATTN // (2 * ATTN_HEAD_DIM)
RWKV_HEAD_DIM = 64
LORA = 64
RMS_EPS = 1e-6
SUBLN_EPS = 1e-5
GN_EPS = 1e-5 * RWKV_HEAD_DIM
A_COLS = 4 * W_ATTN
SHIFT_WIDTH = 3 * W_RWKV + 2 * LORA
B_COLS = SHIFT_WIDTH + W_RWKV
D_IN = A_COLS + B_COLS

LANES = 128
PAIRS = W_RWKV // LANES
VMEM_LIMIT = 56 * 1024 * 1024


def _lambda_init(layer_idx):
    return 0.8 - 0.6 * math.exp(-0.3 * layer_idx)


def _dot(a, b):
    return jnp.dot(a, b, preferred_element_type=F32)


def _dot_nt(a, b):
    return lax.dot_general(a, b, (((1,), (1,)), ((), ())), preferred_element_type=F32)


PRE_NAMES = ("r", "kk", "kka", "km", "vm", "bonus", "gate")
PREP_ROWS = 256


def _split3(x):
    hi = x.astype(BF16)
    r1 = x - hi.astype(F32)
    mid = r1.astype(BF16)
    lo = (r1 - mid.astype(F32)).astype(BF16)
    return hi, mid, lo


def _rwkv_prepare(z, prev_row, mu_ref, w0_ref, w2_ref, a0_ref, a2_ref, kk_ref, ka_ref, rk_ref):
    hd = RWKV_HEAD_DIM
    rows = z.shape[0]
    zs = z[:, :SHIFT_WIDTH]
    row = lax.broadcasted_iota(jnp.int32, (rows, 1), 0)
    prev = jnp.where(row == 0, prev_row, pltpu.roll(zs, 1, axis=0))
    xs = zs + (prev - zs) * mu_ref[...]

    r = xs[:, 0:W_RWKV]
    kb = xs[:, W_RWKV:2 * W_RWKV]
    vb = xs[:, 2 * W_RWKV:3 * W_RWKV]
    lora_in = xs[:, 3 * W_RWKV:SHIFT_WIDTH]
    g = z[:, SHIFT_WIDTH:]

    u = w0_ref[...] + _dot(jnp.tanh(lora_in).astype(BF16), w2_ref[...])
    w_log = -(jnp.maximum(-u, 0.0) + jnp.log(1.0 + jnp.exp(-jnp.abs(u)))) - 0.5
    logw = -jnp.exp(w_log)
    a = 1.0 / (1.0 + jnp.exp(-(a0_ref[...] + _dot(lora_in.astype(BF16), a2_ref[...]))))

    lane4 = lax.broadcasted_iota(jnp.int32, (1, W_RWKV), 1)
    first4 = (lane4 % LANES) < hd

    def head_sum(x):
        outs = []
        for p in range(PAIRS):
            xp = x[:, p * LANES:(p + 1) * LANES]
            fp = first4[:, p * LANES:(p + 1) * LANES]
            s0 = jnp.sum(jnp.where(fp, xp, 0.0), axis=-1, keepdims=True)
            s1 = jnp.sum(jnp.where(fp, 0.0, xp), axis=-1, keepdims=True)
            outs.append(jnp.where(fp, s0, s1))
        return jnp.concatenate(outs, axis=1)

    kk = kb * kk_ref[...]
    kk = kk * lax.rsqrt(jnp.maximum(head_sum(kk * kk), 1e-24))
    k_mod = kb * (1.0 + (a - 1.0) * ka_ref[...])
    bonus = head_sum(r * k_mod * rk_ref[...]) * vb
    pre = dict(r=r, kk=kk, kka=kk * a, km=k_mod, vm=vb, bonus=bonus,
               gate=g / (1.0 + jnp.exp(-g)))
    return pre, logw


def _decay_operands(pre, logw, nchunk):
    c = CHUNK
    rows = nchunk * c
    ri = lax.broadcasted_iota(jnp.int32, (rows, rows), 0)
    ci = lax.broadcasted_iota(jnp.int32, (rows, rows), 1)
    tri = ((ci <= ri) & ((ci // c) == (ri // c))).astype(BF16)
    hi, mid, lo = _split3(logw)
    log2e = math.log2(math.e)
    cs = (_dot(tri, hi) + _dot(tri, mid) + _dot(tri, lo)) * log2e
    c_ends = [cs[k * c + c - 1:k * c + c, :] for k in range(nchunk)]
    c_end = jnp.concatenate([jnp.broadcast_to(ce, (c, W_RWKV)) for ce in c_ends], axis=0)
    e_neg = jnp.exp2(-cs)
    e_end = jnp.exp2(c_end - cs)
    at = -pre["kk"] * jnp.exp2(cs - logw * log2e)
    rt = pre["r"] * jnp.exp2(cs)
    return (at, rt, pre["kka"] * e_neg, pre["km"] * e_neg, pre["kka"] * e_end, pre["km"] * e_end,
            [jnp.exp2(ce) for ce in c_ends])


def _inproj_kernel(x_ref, g_ref, wa_ref, wb_ref, mu_ref, w0_ref, w2_ref, a0_ref, a2_ref, kk_ref,
                   ka_ref, rk_ref, oa_ref, pre_ref, lw_ref, prev_scr, *, steps_per_seq):
    tm = x_ref.shape[0]
    q_scale = ATTN_HEAD_DIM ** -0.5 * math.log2(math.e)

    @pl.when(pl.program_id(0) % steps_per_seq == 0)
    def _():
        prev_scr[...] = jnp.zeros_like(prev_scr)

    def normed(n):
        x = x_ref[n * PREP_ROWS:(n + 1) * PREP_ROWS, :]
        ms = jnp.mean(x * x, axis=-1, keepdims=True)
        return (x * lax.rsqrt(ms + RMS_EPS) * g_ref[...]).astype(BF16)

    def project_attn(n, xn):
        rows = slice(n * PREP_ROWS, (n + 1) * PREP_ROWS)
        oa_ref[rows, :W_ATTN] = (_dot(xn, wa_ref[:, :W_ATTN]) * q_scale).astype(oa_ref.dtype)
        oa_ref[rows, W_ATTN:] = _dot(xn, wa_ref[:, W_ATTN:]).astype(oa_ref.dtype)

    def prepare(n, z, prev_row):
        rows = slice(n * PREP_ROWS, (n + 1) * PREP_ROWS)
        pre, logw = _rwkv_prepare(z, prev_row, mu_ref, w0_ref, w2_ref, a0_ref, a2_ref, kk_ref,
                                  ka_ref, rk_ref)
        for idx, name in enumerate(PRE_NAMES):
            pre_ref[idx, rows, :] = pre[name].astype(pre_ref.dtype)
        lw_ref[rows, :] = logw

    nsub = tm // PREP_ROWS
    prev_row = prev_scr[...]
    xn = normed(0)
    z = _dot(xn, wb_ref[...])
    for n in range(nsub):
        if n + 1 < nsub:
            xn_next = normed(n + 1)
            z_next = _dot(xn_next, wb_ref[...])
        prepare(n, z, prev_row)
        project_attn(n, xn)
        prev_row = z[PREP_ROWS - 1:PREP_ROWS, :SHIFT_WIDTH]
        if n + 1 < nsub:
            xn, z = xn_next, z_next
    prev_scr[...] = prev_row


def _inproj(x2d, norm_g, w_a, w_b, mu, w0, w2p, a0, a2p, k_k, k_a, r_k, tm, seq):
    m = x2d.shape[0]
    vec = lambda n: pl.BlockSpec((1, n), lambda i: (0, 0))
    mat = pl.BlockSpec((LANES, W_RWKV), lambda i: (0, 0))
    return pl.pallas_call(
        functools.partial(_inproj_kernel, steps_per_seq=seq // tm),
        grid=(m // tm,),
        in_specs=[
            pl.BlockSpec((tm, D_MODEL), lambda i: (i, 0)),
            vec(D_MODEL),
            pl.BlockSpec((D_MODEL, A_COLS), lambda i: (0, 0)),
            pl.BlockSpec((D_MODEL, B_COLS), lambda i: (0, 0)),
            vec(SHIFT_WIDTH), vec(W_RWKV), mat, vec(W_RWKV), mat,
            vec(W_RWKV), vec(W_RWKV), vec(W_RWKV),
        ],
        out_specs=[
            pl.BlockSpec((tm, A_COLS), lambda i: (i, 0)),
            pl.BlockSpec((len(PRE_NAMES), tm, W_RWKV), lambda i: (0, i, 0)),
            pl.BlockSpec((tm, W_RWKV), lambda i: (i, 0)),
        ],
        out_shape=[
            jax.ShapeDtypeStruct((m, A_COLS), BF16),
            jax.ShapeDtypeStruct((len(PRE_NAMES), m, W_RWKV), BF16),
            jax.ShapeDtypeStruct((m, W_RWKV), F32),
        ],
        scratch_shapes=[pltpu.VMEM((1, SHIFT_WIDTH), F32)],
        compiler_params=pltpu.CompilerParams(
            dimension_semantics=("arbitrary",), vmem_limit_bytes=VMEM_LIMIT),
        name="inproj",
    )(x2d, norm_g, w_a, w_b, mu, w0, w2p, a0, a2p, k_k, k_a, r_k)


ATTN_GROUP = 2


def _attn_kernel(lam_ref, q_ref, k_ref, v_ref, g_ref, sg_ref, o_ref, s_scr, m_scr, acc_scr,
                 *, tq, lam_init):
    i = pl.program_id(2)
    lane = lax.broadcasted_iota(jnp.int32, (1, LANES), 1)
    first = lane < ATTN_HEAD_DIM
    heads = range(ATTN_GROUP)

    lp = lam_ref[...]
    l1 = jnp.sum(lp[0:1] * lp[1:2], axis=-1, keepdims=True)
    l2 = jnp.sum(lp[2:3] * lp[3:4], axis=-1, keepdims=True)
    lam = jnp.exp(l1) - jnp.exp(l2) + lam_init

    def head(ref, hh, rows=slice(None)):
        return ref[0, rows, hh * LANES:(hh + 1) * LANES]

    qs = []
    for hh in heads:
        q = head(q_ref, hh)
        zero = jnp.zeros_like(q)
        qs.append(jnp.concatenate([jnp.where(first, q, zero), jnp.where(first, zero, q)], axis=0))

    nt = tq // LANES

    def tiles(s):
        return [s[:, t * LANES:(t + 1) * LANES] for t in range(nt)]

    def scores(hh, j):
        start = pl.multiple_of(j * tq, tq)
        s = _dot_nt(qs[hh], head(k_ref, hh, pl.ds(start, tq)))
        s_scr[hh, j] = s
        return functools.reduce(jnp.maximum, tiles(s))

    diag = pl.multiple_of(i * tq, tq)
    sub_row = lax.broadcasted_iota(jnp.int32, (2 * LANES, 1), 0) % LANES
    hidden = (sub_row < CHUNK) & (lane >= CHUNK)

    def sub_rows(u):
        return slice(u * LANES, (u + 1) * LANES), slice(tq + u * LANES, tq + (u + 1) * LANES)

    def diag_scores(hh):
        for u in range(nt):
            r1, r2 = sub_rows(u)
            nk = (u + 1) * LANES
            s = _dot_nt(jnp.concatenate([qs[hh][r1], qs[hh][r2]], axis=0),
                        head(k_ref, hh, pl.ds(diag, nk)))
            parts = [s[:, t * LANES:(t + 1) * LANES] for t in range(u)]
            parts.append(jnp.where(hidden, -jnp.inf, s[:, u * LANES:nk]))
            for t, part in enumerate(parts):
                s_scr[hh, i, r1, t * LANES:(t + 1) * LANES] = part[:LANES]
                s_scr[hh, i, r2, t * LANES:(t + 1) * LANES] = part[LANES:]
            m = functools.reduce(jnp.maximum, parts)
            m_scr[hh, r1] = m[:LANES]
            m_scr[hh, r2] = m[LANES:]

    for hh in heads:
        diag_scores(hh)

    def sweep(block_fn):
        n_quads = lax.shift_right_logical(i, 2)

        def body(jj, carry):
            for d in range(4):
                block_fn(4 * jj + d)
            return carry

        lax.fori_loop(0, n_quads, body, 0)
        base = 4 * n_quads

        @pl.when(jnp.bitwise_and(i, 2) == 2)
        def _():
            block_fn(base)
            block_fn(base + 1)

        @pl.when(jnp.bitwise_and(i, 1) == 1)
        def _():
            block_fn(i - 1)

    def pass1_block(j):
        for hh in heads:
            m_scr[hh] = jnp.maximum(m_scr[hh], scores(hh, j))

    sweep(pass1_block)
    mb = [jnp.broadcast_to(jnp.max(m_scr[hh], axis=-1, keepdims=True), (2 * tq, LANES))
          for hh in heads]

    def with_ones(v):
        return jnp.concatenate([v, jnp.ones_like(v)], axis=1)

    def probs_v(hh, j):
        start = pl.multiple_of(j * tq, tq)
        ps = [jnp.exp2(t - mb[hh]).astype(BF16) for t in tiles(s_scr[hh, j])]
        return _dot(jnp.concatenate(ps, axis=1), with_ones(head(v_ref, hh, pl.ds(start, tq))))

    def diag_probs_v(hh):
        for u in range(nt):
            r1, r2 = sub_rows(u)
            nk = (u + 1) * LANES
            mbu = jnp.concatenate([mb[hh][r1], mb[hh][r2]], axis=0)
            ps = [jnp.exp2(jnp.concatenate([s_scr[hh, i, r1, t * LANES:(t + 1) * LANES],
                                            s_scr[hh, i, r2, t * LANES:(t + 1) * LANES]], axis=0)
                           - mbu).astype(BF16) for t in range(u + 1)]
            pv = _dot(jnp.concatenate(ps, axis=1),
                      with_ones(head(v_ref, hh, pl.ds(diag, nk))))
            acc_scr[hh, r1], acc_scr[hh, r2] = pv[:LANES], pv[LANES:]

    for hh in heads:
        diag_probs_v(hh)

    def pass2_block(j):
        for hh in heads:
            acc_scr[hh] += probs_v(hh, j)

    sweep(pass2_block)

    for hh in heads:
        acc = acc_scr[hh]
        o = acc[:, :LANES] / acc[:, LANES:]
        o = o[:tq] - lam * o[tq:]
        ms = jnp.mean(o * o, axis=-1, keepdims=True)
        o = o * lax.rsqrt(ms + SUBLN_EPS) * sg_ref[...] * (1.0 - lam_init)
        g = head(g_ref, hh).astype(F32)
        o_ref[0, :, hh * LANES:(hh + 1) * LANES] = (
            o * (g / (1.0 + jnp.exp(-g)))).astype(o_ref.dtype)


def _attention(proj_a, lam_params, subln_g, bsz, seq, tq, lam_init):
    ng = ATTN_HEADS // ATTN_GROUP
    gw = ATTN_GROUP * LANES
    kernel = functools.partial(_attn_kernel, tq=tq, lam_init=lam_init)
    return pl.pallas_call(
        kernel,
        grid=(bsz, ng, seq // tq),
        in_specs=[
            pl.BlockSpec((4, ATTN_HEAD_DIM), lambda b, h, i: (0, 0)),
            pl.BlockSpec((1, tq, gw), lambda b, h, i: (b, i, h)),
            pl.BlockSpec((1, seq, gw), lambda b, h, i: (b, 0, ng + h)),
            pl.BlockSpec((1, seq, gw), lambda b, h, i: (b, 0, 2 * ng + h)),
            pl.BlockSpec((1, tq, gw), lambda b, h, i: (b, i, 3 * ng + h)),
            pl.BlockSpec((1, LANES), lambda b, h, i: (0, 0)),
        ],
        out_specs=pl.BlockSpec((1, tq, gw), lambda b, h, i: (b, i, h)),
        out_shape=jax.ShapeDtypeStruct((bsz, seq, W_ATTN), BF16),
        scratch_shapes=[pltpu.VMEM((ATTN_GROUP, seq // tq, 2 * tq, tq), F32),
                        pltpu.VMEM((ATTN_GROUP, 2 * tq, LANES), F32),
                        pltpu.VMEM((ATTN_GROUP, 2 * tq, 2 * LANES), F32)],
        compiler_params=pltpu.CompilerParams(
            dimension_semantics=("parallel", "parallel", "arbitrary"),
            vmem_limit_bytes=VMEM_LIMIT),
        name="diff_attn",
    )(lam_params, proj_a, proj_a, proj_a, proj_a, subln_g)


def _rwkv_kernel(pre_ref, lw_ref, lng_ref, lnb_ref, x_ref, oa_ref, woa_ref, wob_ref, fg_ref,
                 o_ref, state_ref, ob_scr, *, nchunk, final):
    c = CHUNK
    hd = RWKV_HEAD_DIM

    @pl.when(pl.program_id(1) == 0)
    def _():
        state_ref[...] = jnp.zeros_like(state_ref)

    h_attn = x_ref[0] + _dot(oa_ref[0], woa_ref[...])

    pre = {name: pre_ref[idx, 0].astype(F32) for idx, name in enumerate(PRE_NAMES)}
    vb, bonus, gate = pre["vm"], pre["bonus"], pre["gate"]
    at_all, rt_all, bt_all, kt_all, bp_all, kp_all, pc_all = _decay_operands(
        pre, lw_ref[0], nchunk)

    lane = lax.broadcasted_iota(jnp.int32, (1, LANES), 1)
    first = lane < hd
    first2 = jnp.concatenate([first, first], axis=1)
    rowc = lax.broadcasted_iota(jnp.int32, (c, LANES), 0)
    colc = lax.broadcasted_iota(jnp.int32, (c, LANES), 1) % hd
    strict = colc < rowc
    incl = colc <= rowc
    r2 = lax.broadcasted_iota(jnp.int32, (LANES, LANES), 0)
    c2 = lax.broadcasted_iota(jnp.int32, (LANES, LANES), 1)
    same_head = (r2 // hd) == (c2 // hd)
    eye = r2 == c2

    def lower_blk(s):
        return ((r2 // (2 * s)) == (c2 // (2 * s))) & ((r2 // s) % 2 == 1) & ((c2 // s) % 2 == 0)
    zeros_cl = jnp.zeros((c, LANES), F32)

    def h0(x):
        return jnp.where(first, x, 0.0)

    def h1(x):
        return jnp.where(first, 0.0, x)

    units = [(k, p) for k in range(nchunk) for p in range(PAIRS)]

    def cut(x, k, p):
        return x[k * c:(k + 1) * c, p * LANES:(p + 1) * LANES]

    at = [cut(at_all, k, p) for k, p in units]
    rt = [cut(rt_all, k, p) for k, p in units]
    vm = [cut(vb, k, p) for k, p in units]

    mm = []
    for n, (k, p) in enumerate(units):
        bt, kt = cut(bt_all, k, p), cut(kt_all, k, p)
        ar = jnp.concatenate([at[n], rt[n]], axis=0).astype(BF16)
        rh = jnp.concatenate([h0(bt), h0(kt), h1(kt), h1(bt)], axis=0).astype(BF16)
        mm.append(_dot_nt(ar, rh))
    la0 = [jnp.where(strict, m[:c, :LANES], 0.0) for m in mm]
    la1 = [jnp.where(strict, m[:c, LANES:], 0.0) for m in mm]
    mr0 = [jnp.where(incl, m[c:, :LANES], 0.0) for m in mm]
    mr1 = [jnp.where(incl, m[c:, LANES:], 0.0) for m in mm]

    lakv = [_dot(jnp.where(first, la1[n], la0[n]).astype(BF16),
                 jnp.concatenate([h1(vm[n]), h0(vm[n])], axis=0).astype(BF16))
            for n in range(len(units))]

    lbd = [jnp.concatenate([h0(la0[n]), h1(la1[n])], axis=0) for n in range(len(units))]
    tinv = [(jnp.where(eye, 1.0, 0.0) + jnp.where(lower_blk(1), l, 0.0)).astype(BF16) for l in lbd]
    lbd16 = [l.astype(BF16) for l in lbd]
    sz = 2
    while sz < c:
        msk = lower_blk(sz)
        inner = [_dot(l, t).astype(BF16) for l, t in zip(lbd16, tinv)]
        tinv = [jnp.where(msk, _dot(t, m).astype(BF16), t) for t, m in zip(tinv, inner)]
        sz *= 2
    tz = []
    for n in range(len(units)):
        zc = jnp.concatenate([at[n], lakv[n]], axis=1)
        x = jnp.concatenate([jnp.where(first2, zc, 0.0), jnp.where(first2, 0.0, zc)], axis=0)
        xs2 = _dot(tinv[n], x.astype(BF16))
        tz.append(xs2[:c] + xs2[c:])
    w = [t[:, :LANES] for t in tz]
    u0 = [t[:, LANES:] for t in tz]

    gmat, hmat = [], []
    for n, (k, p) in enumerate(units):
        bp, kp = cut(bp_all, k, p), cut(kp_all, k, p)
        lt = jnp.concatenate([bp, kp], axis=0).T.astype(BF16)
        rt2 = jnp.concatenate([jnp.concatenate([w[n], u0[n]], axis=1),
                               jnp.concatenate([zeros_cl, vm[n]], axis=1)], axis=0).astype(BF16)
        gh = _dot(lt, rt2)
        pc = pc_all[k][:, p * LANES:(p + 1) * LANES]
        gmat.append(jnp.where(same_head, gh[:, :LANES], 0.0) + jnp.where(eye, pc, 0.0))
        hmat.append(jnp.where(same_head, gh[:, LANES:], 0.0))

    qh, y0 = [], []
    for n in range(len(units)):
        lq = jnp.concatenate([jnp.where(first, mr0[n], mr1[n]), jnp.where(first, mr1[n], mr0[n])],
                             axis=1).astype(BF16)
        rq = jnp.concatenate([
            jnp.concatenate([h0(w[n]), h0(u0[n])], axis=1),
            jnp.concatenate([h1(w[n]), h1(u0[n])], axis=1),
            jnp.concatenate([zeros_cl, h1(vm[n])], axis=1),
            jnp.concatenate([zeros_cl, h0(vm[n])], axis=1)], axis=0).astype(BF16)
        qy = _dot(lq, rq)
        qh.append(rt[n] + qy[:, :LANES])
        y0.append(qy[:, LANES:])

    state = [state_ref[p] for p in range(PAIRS)]
    ys = []
    for n, (k, p) in enumerate(units):
        gs = _dot(jnp.concatenate([gmat[n], qh[n]], axis=0).astype(BF16), state[p].astype(BF16))
        state[p] = gs[:LANES] + hmat[n]
        ys.append(gs[LANES:] + y0[n])
    for p in range(PAIRS):
        state_ref[p] = state[p]

    def pair_mean(t):
        s0 = jnp.sum(h0(t), axis=-1, keepdims=True)
        s1 = jnp.sum(h1(t), axis=-1, keepdims=True)
        return jnp.where(first, s0, s1) * (1.0 / hd)

    for n, (k, p) in enumerate(units):
        sl = slice(p * LANES, (p + 1) * LANES)
        y = ys[n]
        mu = pair_mean(y)
        d = y - mu
        var = pair_mean(d * d)
        yn = d * lax.rsqrt(var + GN_EPS) * lng_ref[:, sl] + lnb_ref[:, sl]
        ob_scr[k * c:(k + 1) * c, sl] = (
            (yn + cut(bonus, k, p)) * cut(gate, k, p)).astype(ob_scr.dtype)

    h = h_attn + _dot(ob_scr[...], wob_ref[...])
    if final:
        ms = jnp.mean(h * h, axis=-1, keepdims=True)
        h = h * lax.rsqrt(ms + RMS_EPS) * fg_ref[...]
    o_ref[0] = h


def _rwkv_out(pre, logw, ln_g, ln_b, x, o_a, w_oa, w_ob, final_g, bsz, seq, nchunk, final):
    tt = nchunk * CHUNK
    vec = lambda n: pl.BlockSpec((1, n), lambda b, t: (0, 0))
    tok = lambda n: pl.BlockSpec((1, tt, n), lambda b, t: (b, t, 0))
    return pl.pallas_call(
        functools.partial(_rwkv_kernel, nchunk=nchunk, final=final),
        grid=(bsz, seq // tt),
        in_specs=[
            pl.BlockSpec((len(PRE_NAMES), 1, tt, W_RWKV), lambda b, t: (0, b, t, 0)),
            tok(W_RWKV), vec(W_RWKV), vec(W_RWKV),
            tok(D_MODEL), tok(W_ATTN),
            pl.BlockSpec((W_ATTN, D_MODEL), lambda b, t: (0, 0)),
            pl.BlockSpec((W_RWKV, D_MODEL), lambda b, t: (0, 0)),
            vec(D_MODEL),
        ],
        out_specs=tok(D_MODEL),
        out_shape=jax.ShapeDtypeStruct((bsz, seq, D_MODEL), F32),
        scratch_shapes=[pltpu.VMEM((PAIRS, LANES, LANES), F32),
                        pltpu.VMEM((tt, W_RWKV), BF16)],
        compiler_params=pltpu.CompilerParams(
            dimension_semantics=("parallel", "arbitrary"), vmem_limit_bytes=VMEM_LIMIT),
        name="rwkv7_out",
    )(pre, logw, ln_g, ln_b, x, o_a, w_oa, w_ob, final_g)


def _pad_rows(w, top):
    zeros = jnp.zeros_like(w)
    return jnp.concatenate([w, zeros] if top else [zeros, w], axis=0)


def kernel(x, norm_g, w_in, lambda_q1, lambda_k1, lambda_q2, lambda_k2, subln_g,
           shift_mu, w0, w2, a0, a2, k_k, k_a, r_k, ln_x_g, ln_x_b, w_out, final_g):
    bsz, seq, d = x.shape
    depth = w_in.shape[0]
    assert d == D_MODEL and w_in.shape[2] == D_IN
    m = bsz * seq
    tm_in = min(1024, seq)
    tq = min(512, seq)
    nchunk = 4
    assert seq % tm_in == 0 and tm_in % PREP_ROWS == 0
    assert seq % tq == 0 and seq % (nchunk * CHUNK) == 0

    row = lambda v: v.reshape(1, -1).astype(F32)
    h = x.reshape(m, d)
    for l in range(depth):
        proj_a, pre, logw = _inproj(
            h, row(norm_g[l]), w_in[l, :, :A_COLS].astype(BF16), w_in[l, :, A_COLS:].astype(BF16),
            row(shift_mu[l]), row(w0[l]), _pad_rows(w2[l], True).astype(BF16), row(a0[l]),
            _pad_rows(a2[l], False).astype(BF16), row(k_k[l]), row(k_a[l]), row(r_k[l]),
            tm_in, seq)
        lam_params = jnp.stack([lambda_q1[l], lambda_k1[l], lambda_q2[l], lambda_k2[l]]).astype(F32)
        o_a = _attention(proj_a.reshape(bsz, seq, A_COLS), lam_params, row(subln_g[l]),
                         bsz, seq, tq, _lambda_init(l))
        h = _rwkv_out(pre.reshape(len(PRE_NAMES), bsz, seq, W_RWKV),
                      logw.reshape(bsz, seq, W_RWKV), row(ln_x_g[l]), row(ln_x_b[l]),
                      h.reshape(bsz, seq, d), o_a, w_out[l, :W_ATTN].astype(BF16),
                      w_out[l, W_ATTN:].astype(BF16), row(final_g), bsz, seq, nchunk,
                      final=(l == depth - 1)).reshape(m, d)
    return h.reshape(bsz, seq, d)
```
